```python
import math
import jax
import jax.numpy as jnp
from jax import lax
import numpy as np

D_MODEL = 1024
BATCH = 8
SEQ = 2048
DEPTH = 4
DEC_BATCH = 128
DEC_SEQ = 8
PAST_LEN = 2048
PAGE_SIZE = 128

HEAD_DIM = 64
N_HEADS = 12
ATTN_DIM = N_HEADS * HEAD_DIM
ATTN_SCALE = HEAD_DIM ** -0.5
N_MIXERS = 2
N_A = (DEPTH + 1) // 2
N_B = DEPTH // 2
MOBA_BLOCK = 256
MOBA_TOPK = 3
MOBA_Q_CHUNK = 32
DIL_GROUPS = ((128, 1), (512, 4), (2048, 16))
N_GROUPS = 3
GROUP_HEADS = N_HEADS // N_GROUPS
DIL_Q_CHUNK = 128
REL_BUCKETS = 32
REL_MAX_DIST = 2048
PEER_HEADS = 8
PEER_NKEYS = 128
PEER_N_EXPERTS = PEER_NKEYS * PEER_NKEYS
PEER_DK = 256
PEER_TOPK = 16
PEER_CHUNK = 128
RMS_EPS = 1e-6
F32 = jnp.float32
NEG_INF = float("-inf")

kernel_name = "moba_dilated_peer_hybrid_step"


def rms_norm(x, g):
    x32 = x.astype(F32)
    y = x32 * lax.rsqrt(jnp.mean(x32 * x32, axis=-1, keepdims=True) + RMS_EPS)
    return y.astype(x.dtype) * g


def ada_params(c, w, b):
    return (jax.nn.silu(c) @ w + b).reshape(c.shape[0], 6, D_MODEL)


def modulate(x, shift, scale):
    return x * (1 + scale[:, None, :]) + shift[:, None, :]


def qkv_split(a, w):
    qkv = (a @ w).reshape(a.shape[0], a.shape[1], 3, N_HEADS, HEAD_DIM)
    return qkv[:, :, 0], qkv[:, :, 1], qkv[:, :, 2]


def rel_bucket(dist):
    n = jnp.maximum(dist, 0)
    max_exact = REL_BUCKETS // 2
    nf = jnp.maximum(n, 1).astype(F32)
    large = max_exact + (jnp.log(nf / max_exact) / math.log(REL_MAX_DIST / max_exact)
                         * (REL_BUCKETS - max_exact)).astype(jnp.int32)
    large = jnp.minimum(large, REL_BUCKETS - 1)
    return jnp.where(n < max_exact, n, large)


def strided_attend(q, k, v, bias, valid):
    l = jnp.einsum("bqhd,bqkhd->bqhk", q, k).astype(F32) * ATTN_SCALE + bias.astype(F32)
    l = jnp.where(valid[None, :, None, :], l, NEG_INF)
    lse = jax.nn.logsumexp(l, axis=-1)
    p = jnp.exp(l - lse[..., None]).astype(v.dtype)
    return jnp.einsum("bqhk,bqkhd->bqhd", p, v), lse


def combine_groups(outs, lses):
    alpha = jax.nn.softmax(jnp.stack(lses, 0), axis=0)
    return jnp.concatenate([o * alpha[g][..., None].astype(o.dtype) for g, o in enumerate(outs)], axis=2)


def dilated_prompt(q, k, v, rbt):
    B, S = q.shape[:2]
    starts = jnp.arange(S // DIL_Q_CHUNK) * DIL_Q_CHUNK
    outs, lses, states = [], [], []
    for g, (win, dil) in enumerate(DIL_GROUPS):
        hs = slice(g * GROUP_HEADS, (g + 1) * GROUP_HEADS)
        qg, kg, vg = q[:, :, hs], k[:, :, hs], v[:, :, hs]
        offs = dil * jnp.arange(win // dil + 1)
        bias = rbt[hs][:, rel_bucket(offs)]

        def chunk(start, qg=qg, kg=kg, vg=vg, offs=offs, bias=bias):
            qc = lax.dynamic_slice_in_dim(qg, start, DIL_Q_CHUNK, axis=1)
            kpos = (start + jnp.arange(DIL_Q_CHUNK))[:, None] - offs[None, :]
            idx = jnp.maximum(kpos, 0)
            return strided_attend(qc, jnp.take(kg, idx, axis=1), jnp.take(vg, idx, axis=1), bias, kpos >= 0)

        o, lse = lax.map(chunk, starts)
        outs.append(o.swapaxes(0, 1).reshape(B, S, GROUP_HEADS, HEAD_DIM))
        lses.append(lse.swapaxes(0, 1).reshape(B, S, GROUP_HEADS))
        keep = min(win, S)
        states.append(jnp.stack([kg, vg], axis=2)[:, S - keep:])
    return combine_groups(outs, lses), states


def dilated_sample(q, k, v, bufs, rbt):
    DB, T = q.shape[:2]
    pos = PAST_LEN + jnp.arange(T)
    outs, lses, states = [], [], []
    for g, (win, dil) in enumerate(DIL_GROUPS):
        hs = slice(g * GROUP_HEADS, (g + 1) * GROUP_HEADS)
        buf = bufs[g]
        lb = buf.shape[1]
        kv = jnp.concatenate([buf, jnp.stack([k[:, :, hs], v[:, :, hs]], axis=2)], axis=1)
        offs = dil * jnp.arange(win // dil + 1)
        idx = pos[:, None] - offs[None, :] - (PAST_LEN - lb)
        valid = idx >= 0
        idx = jnp.maximum(idx, 0)
        bias = rbt[hs][:, rel_bucket(offs)]
        o, lse = strided_attend(q[:, :, hs], jnp.take(kv[:, :, 0], idx, axis=1),
                                jnp.take(kv[:, :, 1], idx, axis=1), bias, valid)
        outs.append(o)
        lses.append(lse)
        keep = min(win, PAST_LEN + T)
        states.append(kv[:, kv.shape[1] - keep:])
    return combine_groups(outs, lses), states


def moba_prompt(q, k, v, rbt):
    B, S, H, hd = q.shape
    nb = -(-S // MOBA_BLOCK)
    pad = nb * MOBA_BLOCK - S
    kpad = jnp.pad(k, ((0, 0), (0, pad), (0, 0), (0, 0)))
    vpad = jnp.pad(v, ((0, 0), (0, pad), (0, 0), (0, 0)))
    kb = kpad.reshape(B, nb, MOBA_BLOCK, H, hd)
    vb = vpad.reshape(B, nb, MOBA_BLOCK, H, hd)
    kmean = jnp.mean(kb.astype(F32), axis=2)
    n_sel = min(MOBA_TOPK, nb)
    n_s = n_sel * MOBA_BLOCK
    blk = jnp.arange(MOBA_BLOCK)
    b_idx = jnp.arange(B)[:, None, None, None]
    h_idx = jnp.arange(H)[None, None, :, None]

    def chunk(start):
        qc = lax.dynamic_slice_in_dim(q, start, MOBA_Q_CHUNK, axis=1)
        pos = start + jnp.arange(MOBA_Q_CHUNK)
        own = start // MOBA_BLOCK
        s = jnp.einsum("bqhd,bnhd->bqhn", qc.astype(F32), kmean)
        s = jnp.where(jnp.arange(nb) < own, s, NEG_INF)
        _, sel = lax.top_k(s, n_sel)
        k_sel = kb[b_idx, sel, :, h_idx]
        v_sel = vb[b_idx, sel, :, h_idx]
        kpos = sel[..., None] * MOBA_BLOCK + blk
        l_sel = (jnp.einsum("bqhd,bqhnld->bqhnl", qc, k_sel).astype(F32) * ATTN_SCALE
                 + rbt[h_idx[..., None], rel_bucket(pos[None, :, None, None, None] - kpos)])
        l_sel = jnp.where((sel < own)[..., None], l_sel, NEG_INF)
        k_own = lax.dynamic_slice_in_dim(kpad, own * MOBA_BLOCK, MOBA_BLOCK, axis=1)
        v_own = lax.dynamic_slice_in_dim(vpad, own * MOBA_BLOCK, MOBA_BLOCK, axis=1)
        d_own = pos[:, None] - (own * MOBA_BLOCK + blk)[None, :]
        l_own = (jnp.einsum("bqhd,blhd->bqhl", qc, k_own).astype(F32) * ATTN_SCALE
                 + rbt[:, rel_bucket(d_own)].transpose(1, 0, 2)[None])
        l_own = jnp.where((d_own >= 0)[None, :, None, :], l_own, NEG_INF)
        logits = jnp.concatenate([l_sel.reshape(B, MOBA_Q_CHUNK, H, n_s), l_own], axis=-1)
        p = jax.nn.softmax(logits, axis=-1).astype(v.dtype)
        return (jnp.einsum("bqhnl,bqhnld->bqhd", p[..., :n_s].reshape(l_sel.shape), v_sel)
                + jnp.einsum("bqhl,blhd->bqhd", p[..., n_s:], v_own))

    o = lax.map(chunk, jnp.arange(S // MOBA_Q_CHUNK) * MOBA_Q_CHUNK)
    return o.swapaxes(0, 1).reshape(B, S, H, hd)


def moba_sample(q, k_new, v_new, pool_k, pool_v, page_table, rbt):
    DB, T, H, hd = q.shape
    n_full = PAST_LEN // MOBA_BLOCK
    base = n_full * MOBA_BLOCK
    pos = PAST_LEN + jnp.arange(T)
    blk = jnp.arange(MOBA_BLOCK)
    rem = jnp.arange(base, PAST_LEN)
    phys_rem = page_table[:, rem // PAGE_SIZE]
    k_own = jnp.concatenate([pool_k[phys_rem, rem % PAGE_SIZE], k_new], axis=1)
    v_own = jnp.concatenate([pool_v[phys_rem, rem % PAGE_SIZE], v_new], axis=1)
    d_own = pos[:, None] - (base + jnp.arange(k_own.shape[1]))[None, :]
    l_own = (jnp.einsum("bqhd,blhd->bqhl", q, k_own).astype(F32) * ATTN_SCALE
             + rbt[:, rel_bucket(d_own)].transpose(1, 0, 2)[None])
    l_own = jnp.where((d_own >= 0)[None, :, None, :], l_own, NEG_INF)
    if n_full == 0:
        p = jax.nn.softmax(l_own, axis=-1).astype(v_new.dtype)
        return jnp.einsum("bqhl,blhd->bqhd", p, v_own)
    k_full = pool_k[page_table[:, :base // PAGE_SIZE]].reshape(DB, n_full, MOBA_BLOCK, H, hd)
    kmean = jnp.mean(k_full.astype(F32), axis=2)
    s = jnp.einsum("bqhd,bnhd->bqhn", q.astype(F32), kmean)
    n_sel = min(MOBA_TOPK, n_full)
    n_s = n_sel * MOBA_BLOCK
    _, sel = lax.top_k(s, n_sel)
    b_idx = jnp.arange(DB)[:, None, None, None]
    h_idx = jnp.arange(H)[None, :, None, None]

    def per_query(args):
        q_t, sel_t, l_own_t, pos_t = args
        kpos = sel_t[..., None] * MOBA_BLOCK + blk
        phys = page_table[b_idx, kpos // PAGE_SIZE]
        off = kpos % PAGE_SIZE
        k_sel = pool_k[phys, off, h_idx]
        v_sel = pool_v[phys, off, h_idx]
        l_sel = (jnp.einsum("bhd,bhnld->bhnl", q_t, k_sel).astype(F32) * ATTN_SCALE
                 + rbt[h_idx, rel_bucket(pos_t - kpos)])
        logits = jnp.concatenate([l_sel.reshape(DB, H, n_s), l_own_t], axis=-1)
        p = jax.nn.softmax(logits, axis=-1).astype(v_new.dtype)
        return (jnp.einsum("bhnl,bhnld->bhd", p[..., :n_s].reshape(l_sel.shape), v_sel)
                + jnp.einsum("bhl,blhd->bhd", p[..., n_s:], v_own))

    o = lax.map(per_query, (q.swapaxes(0, 1), sel.swapaxes(0, 1), l_own.swapaxes(0, 1), pos))
    return o.swapaxes(0, 1)


def peer(x, w_pq, sub_keys, u_tab, v_tab):
    n = x.shape[0]
    pad = (-n) % PEER_CHUNK
    xs = jnp.pad(x, ((0, pad), (0, 0))).reshape(-1, PEER_CHUNK, D_MODEL)
    half = PEER_DK // 2
    n_cand = PEER_TOPK * PEER_TOPK

    def chunk(xc):
        qh = (xc @ w_pq).reshape(PEER_CHUNK, PEER_HEADS, 2, half).astype(F32)
        s1 = jnp.einsum("chd,kd->chk", qh[:, :, 0], sub_keys[0].astype(F32))
        s2 = jnp.einsum("chd,kd->chk", qh[:, :, 1], sub_keys[1].astype(F32))
        t1, i1 = lax.top_k(s1, PEER_TOPK)
        t2, i2 = lax.top_k(s2, PEER_TOPK)
        cand = (t1[..., :, None] + t2[..., None, :]).reshape(PEER_CHUNK, PEER_HEADS, n_cand)
        cidx = (i1[..., :, None] * PEER_NKEYS + i2[..., None, :]).reshape(PEER_CHUNK, PEER_HEADS, n_cand)
        top, j = lax.top_k(cand, PEER_TOPK)
        eidx = jnp.take_along_axis(cidx, j, axis=-1)
        gate = jax.nn.softmax(top, axis=-1)
        u = u_tab[eidx]
        v = v_tab[eidx]
        act = jax.nn.gelu(jnp.einsum("cd,chkd->chk", xc, u).astype(F32), approximate=False)
        return jnp.einsum("chk,chkd->cd", (gate * act).astype(x.dtype), v)

    return lax.map(chunk, xs).reshape(-1, D_MODEL)[:n]


def setup_inputs(seed: int = 0) -> dict:
    key = jax.random.key(seed)
    ks = jax.random.split(key, 24)
    n_pages = PAST_LEN // PAGE_SIZE
    n_used = DEC_BATCH * n_pages
    n_pool = n_used + max(n_used // 4, 1)

    def nrm(k, shape, std):
        return jax.random.normal(k, shape, F32) * std

    dil_len = [min(w, PAST_LEN) for w, _ in DIL_GROUPS]
    page_table = jax.random.permutation(ks[7], n_pool)[:n_used].reshape(DEC_BATCH, n_pages).astype(jnp.int32)
    return {
        "x_prompt": nrm(ks[0], (BATCH, SEQ, D_MODEL), 1.0),
        "x_sample": nrm(ks[1], (DEC_BATCH, DEC_SEQ, D_MODEL), 1.0),
        "cache_k_moba": nrm(ks[2], (N_A, n_pool, PAGE_SIZE, N_HEADS, HEAD_DIM), 1.0),
        "cache_v_moba": nrm(ks[3], (N_A, n_pool, PAGE_SIZE, N_HEADS, HEAD_DIM), 1.0),
        "state_kv_dil0": nrm(ks[4], (N_B, DEC_BATCH, dil_len[0], 2, GROUP_HEADS, HEAD_DIM), 1.0),
        "state_kv_dil1": nrm(ks[5], (N_B, DEC_BATCH, dil_len[1], 2, GROUP_HEADS, HEAD_DIM), 1.0),
        "state_kv_dil2": nrm(ks[6], (N_B, DEC_BATCH, dil_len[2], 2, GROUP_HEADS, HEAD_DIM), 1.0),
        "page_table": page_table,
        "c_prompt": nrm(ks[8], (BATCH, D_MODEL), 1.0),
        "c_sample": nrm(ks[9], (DEC_BATCH, D_MODEL), 1.0),
        "w_ada": nrm(ks[10], (DEPTH, D_MODEL, 6 * D_MODEL), 0.5 * D_MODEL ** -0.5),
        "b_ada": nrm(ks[11], (DEPTH, 6 * D_MODEL), 0.02),
        "g_norm": 1.0 + nrm(ks[12], (DEPTH, 2, D_MODEL), 0.02),
        "w_qkv": nrm(ks[13], (DEPTH, D_MODEL, 3 * ATTN_DIM), D_MODEL ** -0.5),
        "w_o": nrm(ks[14], (DEPTH, ATTN_DIM, D_MODEL), ATTN_DIM ** -0.5),
        "rel_bias": nrm(ks[15], (REL_BUCKETS, N_HEADS), 0.5),
        "w_pq": nrm(ks[16], (DEPTH, D_MODEL, PEER_HEADS * PEER_DK), D_MODEL ** -0.5),
        "peer_sub_keys": nrm(ks[17], (DEPTH, 2, PEER_NKEYS, PEER_DK // 2), (PEER_DK // 2) ** -0.5),
        "peer_u": nrm(ks[18], (DEPTH, PEER_N_EXPERTS, D_MODEL), D_MODEL ** -0.5),
        "peer_v": nrm(ks[19], (DEPTH, PEER_N_EXPERTS, D_MODEL), 0.25),
        "g_final": 1.0 + nrm(ks[20], (D_MODEL,), 0.02),
    }


def reference(x_prompt, x_sample, cache_k_moba, cache_v_moba, state_kv_dil0, state_kv_dil1,
              state_kv_dil2, page_table, c_prompt, c_sample, w_ada, b_ada, g_norm, w_qkv, w_o,
              rel_bias, w_pq, peer_sub_keys, peer_u, peer_v, g_final):
    B, S, _ = x_prompt.shape
    DB, T, _ = x_sample.shape
    rbt = rel_bias.T
    dil_states = (state_kv_dil0, state_kv_dil1, state_kv_dil2)
    hp, hs = x_prompt, x_sample
    kp_rows, vp_rows, ks_rows, vs_rows = [], [], [], []
    dil_p = [[] for _ in DIL_GROUPS]
    dil_s = [[] for _ in DIL_GROUPS]
    for i in range(DEPTH):
        j = i // N_MIXERS
        mp = ada_params(c_prompt, w_ada[i], b_ada[i])
        ms = ada_params(c_sample, w_ada[i], b_ada[i])
        qp, kp, vp = qkv_split(modulate(rms_norm(hp, g_norm[i, 0]), mp[:, 0], mp[:, 1]), w_qkv[i])
        qs, ks, vs = qkv_split(modulate(rms_norm(hs, g_norm[i, 0]), ms[:, 0], ms[:, 1]), w_qkv[i])
        if i % N_MIXERS == 0:
            op = moba_prompt(qp, kp, vp, rbt)
            os_ = moba_sample(qs, ks, vs, cache_k_moba[j], cache_v_moba[j], page_table, rbt)
            kp_rows.append(kp)
            vp_rows.append(vp)
            ks_rows.append(ks)
            vs_rows.append(vs)
        else:
            op, st_p = dilated_prompt(qp, kp, vp, rbt)
            os_, st_s = dilated_sample(qs, ks, vs, [st[j] for st in dil_states], rbt)
            for g in range(N_GROUPS):
                dil_p[g].append(st_p[g])
                dil_s[g].append(st_s[g])
        hp = hp + mp[:, 2, None] * (op.reshape(B, S, ATTN_DIM) @ w_o[i])
        hs = hs + ms[:, 2, None] * (os_.reshape(DB, T, ATTN_DIM) @ w_o[i])
        fp = modulate(rms_norm(hp, g_norm[i, 1]), mp[:, 3], mp[:, 4])
        fs = modulate(rms_norm(hs, g_norm[i, 1]), ms[:, 3], ms[:, 4])
        hp = hp + mp[:, 5, None] * peer(fp.reshape(B * S, D_MODEL), w_pq[i], peer_sub_keys[i],
                                        peer_u[i], peer_v[i]).reshape(B, S, D_MODEL)
        hs = hs + ms[:, 5, None] * peer(fs.reshape(DB * T, D_MODEL), w_pq[i], peer_sub_keys[i],
                                        peer_u[i], peer_v[i]).reshape(DB, T, D_MODEL)
    y_prompt = rms_norm(hp, g_final)
    y_sample = rms_norm(hs, g_final)
    return (y_prompt, y_sample,
            jnp.stack(kp_rows), jnp.stack(vp_rows), jnp.stack(ks_rows), jnp.stack(vs_rows),
            jnp.stack(dil_p[0]), jnp.stack(dil_p[1]), jnp.stack(dil_p[2]),
            jnp.stack(dil_s[0]), jnp.stack(dil_s[1]), jnp.stack(dil_s[2]))
```

```python
import functools
import math

import numpy as np
import jax
import jax.numpy as jnp
from jax import lax
from jax.experimental import pallas as pl
from jax.experimental.pallas import tpu as pltpu

F32 = jnp.float32
BF16 = jnp.bfloat16
NEG = -1e30

D_MODEL = 1024
N_HEADS = 12
HEAD_DIM = 64
ATTN_DIM = N_HEADS * HEAD_DIM
QKV_DIM = 3 * ATTN_DIM
ATTN_SCALE = HEAD_DIM ** -0.5
DEPTH = 4
PAGE_SIZE = 128
MOBA_BLOCK = 256
MOBA_TOPK = 3
DIL_GROUPS = ((128, 1), (512, 4), (2048, 16))
GROUP_HEADS = 4
GROUP_DIM = GROUP_HEADS * HEAD_DIM
DIL_CHUNK = 128
REL_BUCKETS = 32
REL_MAX_DIST = 2048
PEER_HEADS = 8
PEER_NKEYS = 128
PEER_DK = 256
PEER_TOPK = 16
RMS_EPS = 1e-6

VMEM_LIMIT = 56 * 1024 * 1024


def _cparams(sem):
    return pltpu.CompilerParams(dimension_semantics=sem, vmem_limit_bytes=VMEM_LIMIT)


def _split(x):
    hi = x.astype(BF16)
    lo = (x - hi.astype(F32)).astype(BF16)
    return hi, lo


def _dot(a, b):
    return jnp.dot(a, b, preferred_element_type=F32)


def _dot_nt(a, b):
    return lax.dot_general(a, b, (((1,), (1,)), ((), ())), preferred_element_type=F32)


def _dot3(a, b_hi, b_lo):
    a_hi, a_lo = _split(a)
    return _dot(a_hi, b_hi) + _dot(a_lo, b_hi) + _dot(a_hi, b_lo)


def _dot3_nt(a, b):
    a_hi, a_lo = _split(a)
    b_hi, b_lo = _split(b)
    return _dot_nt(a_hi, b_hi) + _dot_nt(a_lo, b_hi) + _dot_nt(a_hi, b_lo)


def _norm_mod(x, g, shift, scale):
    ms = jnp.mean(x * x, axis=-1, keepdims=True)
    xn = x * lax.rsqrt(ms + RMS_EPS) * g
    return xn * (1.0 + scale) + shift


def _bucket_np(dist):
    n = np.maximum(dist, 0)
    max_exact = REL_BUCKETS // 2
    nf = np.maximum(n, 1).astype(np.float64)
    large = max_exact + (np.log(nf / max_exact) / math.log(REL_MAX_DIST / max_exact)
                         * (REL_BUCKETS - max_exact)).astype(np.int64)
    large = np.minimum(large, REL_BUCKETS - 1)
    return np.where(n < max_exact, n, large).astype(np.int32)


def _bias_table(rel_bias, head_idx, dist, valid):
    vals = rel_bias[jnp.asarray(_bucket_np(dist)), jnp.asarray(head_idx)]
    return jnp.where(jnp.asarray(valid), vals, NEG).astype(F32)


def _rank_select(s, lane_n, n_valid, topk):
    nb = s.shape[1]
    rank = jnp.zeros(s.shape, F32)
    for m in range(nb):
        sm = s[:, m:m + 1]
        beats = (sm > s) | ((sm == s) & (m < lane_n))
        beats = beats & (m < n_valid)
        rank = rank + jnp.where(beats, 1.0, 0.0)
    sel = (lane_n < n_valid) & (rank < float(topk))
    return jnp.where(sel, 1.0, 0.0)


def _ada_kernel(c_ref, w_ref, b_ref, o_ref):
    c = c_ref[...]
    s = c * (1.0 / (1.0 + jnp.exp(-c)))
    w_hi, w_lo = _split(w_ref[...])
    o_ref[...] = _dot3(s, w_hi, w_lo) + b_ref[...]


def _ada(c_all, w_ada, b_ada):
    n = c_all.shape[0]
    depth, d, n6 = w_ada.shape
    tn = 1536
    return pl.pallas_call(
        _ada_kernel,
        grid=(depth, n6 // tn),
        in_specs=[
            pl.BlockSpec((n, d), lambda i, j: (0, 0)),
            pl.BlockSpec((None, d, tn), lambda i, j: (i, 0, j)),
            pl.BlockSpec((None, 1, tn), lambda i, j: (i, 0, j)),
        ],
        out_specs=pl.BlockSpec((None, n, tn), lambda i, j: (i, 0, j)),
        out_shape=jax.ShapeDtypeStruct((depth, n, n6), F32),
        compiler_params=_cparams(("parallel", "parallel")),
        name="ada",
    )(c_all, w_ada, b_ada.reshape(depth, 1, n6))


def _nmm_kernel(x_ref, g_ref, sh_ref, sc_ref, whi_ref, wlo_ref, o_ref):
    f = _norm_mod(x_ref[...], g_ref[...], sh_ref[...], sc_ref[...])
    o_ref[...] = _dot3(f, whi_ref[...], wlo_ref[...])


def _mod_spec(mod, tm, tiles_per_seg):
    r = mod.shape[1]
    return pl.BlockSpec((None, r, D_MODEL), lambda i, *_: (i // tiles_per_seg, 0, 0))


def _nmm(x, g, shift, scale, w_hi, w_lo, tm, tiles_per_seg):
    n, d = x.shape
    nout = w_hi.shape[1]
    return pl.pallas_call(
        _nmm_kernel,
        grid=(n // tm,),
        in_specs=[
            pl.BlockSpec((tm, d), lambda i: (i, 0)),
            pl.BlockSpec((1, d), lambda i: (0, 0)),
            _mod_spec(shift, tm, tiles_per_seg),
            _mod_spec(scale, tm, tiles_per_seg),
            pl.BlockSpec((d, nout), lambda i: (0, 0)),
            pl.BlockSpec((d, nout), lambda i: (0, 0)),
        ],
        out_specs=pl.BlockSpec((tm, nout), lambda i: (i, 0)),
        out_shape=jax.ShapeDtypeStruct((n, nout), F32),
        compiler_params=_cparams(("parallel",)),
        name="nmm",
    )(x, g.reshape(1, d), shift, scale, w_hi, w_lo)


def _moba_prompt_kernel(q_ref, k_ref, v_ref, bias_ref, o_ref, kmean_ref):
    i = pl.program_id(2)
    blk = MOBA_BLOCK
    nb = k_ref.shape[0] // blk

    @pl.when(i == 0)
    def _():
        for n in range(nb):
            kmean_ref[n:n + 1, :] = jnp.mean(k_ref[n * blk:(n + 1) * blk, :], axis=0, keepdims=True)

    lane_n = lax.broadcasted_iota(jnp.int32, (blk, nb), 1)
    row0 = pl.multiple_of(i * blk, blk)
    for hh in range(2):
        sl = slice(hh * HEAD_DIM, (hh + 1) * HEAD_DIM)
        q = q_ref[:, sl] * ATTN_SCALE
        sel = _rank_select(_dot3_nt(q, kmean_ref[:, sl]), lane_n, i, MOBA_TOPK)
        qb = q.astype(BF16)
        logits = _dot_nt(qb, k_ref[pl.ds(row0, blk), sl].astype(BF16)) + bias_ref[0, hh]
        m = jnp.max(logits, axis=1, keepdims=True)
        p = jnp.exp(logits - m)
        l = jnp.sum(p, axis=1, keepdims=True)
        acc = _dot(p.astype(BF16), v_ref[pl.ds(row0, blk), sl].astype(BF16))

        def body(j, carry):
            m, l, acc = carry
            r0 = pl.multiple_of(j * blk, blk)
            logits = _dot_nt(qb, k_ref[pl.ds(r0, blk), sl].astype(BF16)) + bias_ref[i - j, hh]
            selj = jnp.sum(jnp.where(lane_n == j, sel, 0.0), axis=1, keepdims=True)
            logits = jnp.where(selj > 0.5, logits, NEG)
            m_new = jnp.maximum(m, jnp.max(logits, axis=1, keepdims=True))
            alpha = jnp.exp(m - m_new)
            p = jnp.exp(logits - m_new)
            l = alpha * l + jnp.sum(p, axis=1, keepdims=True)
            acc = alpha * acc + _dot(p.astype(BF16), v_ref[pl.ds(r0, blk), sl].astype(BF16))
            return m_new, l, acc

        m, l, acc = lax.fori_loop(0, i, body, (m, l, acc))
        o_ref[:, sl] = acc / l


def _moba_prompt(qkv, bias):
    b, s, _ = qkv.shape
    blk = MOBA_BLOCK
    nb = s // blk
    npair = N_HEADS // 2
    return pl.pallas_call(
        _moba_prompt_kernel,
        grid=(npair, b, nb),
        in_specs=[
            pl.BlockSpec((None, blk, 128), lambda hp, bb, i: (bb, i, hp)),
            pl.BlockSpec((None, s, 128), lambda hp, bb, i: (bb, 0, npair + hp)),
            pl.BlockSpec((None, s, 128), lambda hp, bb, i: (bb, 0, 2 * npair + hp)),
            pl.BlockSpec((nb, 2, blk, blk), lambda hp, bb, i: (0, hp, 0, 0)),
        ],
        out_specs=pl.BlockSpec((None, blk, 128), lambda hp, bb, i: (bb, i, hp)),
        out_shape=jax.ShapeDtypeStruct((b, s, ATTN_DIM), F32),
        scratch_shapes=[pltpu.VMEM((nb, 128), F32)],
        compiler_params=_cparams(("parallel", "parallel", "arbitrary")),
        name="moba_prompt",
    )(qkv, qkv, qkv, bias)


def _moba_prompt_bias(rel_bias, s):
    blk = MOBA_BLOCK
    nb = s // blk
    delta = np.arange(nb)[:, None, None, None]
    h = np.arange(N_HEADS)[None, :, None, None]
    r = np.arange(blk)[None, None, :, None]
    c = np.arange(blk)[None, None, None, :]
    dist = np.broadcast_to(delta * blk + r - c, (nb, N_HEADS, blk, blk))
    return _bias_table(rel_bias, np.broadcast_to(h, dist.shape), dist, dist >= 0)


def _moba_sample_kernel(pt_ref, new_ref, kp_ref, vp_ref, sbias_ref, o_ref,
                        qhi_ref, qlo_ref, lg_ref, ksum_ref, p_ref, acc_ref, linv_ref, *, n_pages, t_new):
    del pt_ref
    p = pl.program_id(1)
    rows = N_HEADS * t_new
    pages_per_block = MOBA_BLOCK // PAGE_SIZE
    n_blocks = n_pages // pages_per_block
    r_head = lax.broadcasted_iota(jnp.int32, (rows, ATTN_DIM), 0) // t_new
    c_head = lax.broadcasted_iota(jnp.int32, (rows, ATTN_DIM), 1) // HEAD_DIM
    pad = jnp.zeros((PAGE_SIZE - t_new, ATTN_DIM), F32)

    @pl.when(p == 0)
    def _():
        q = new_ref[:, 0:ATTN_DIM] * ATTN_SCALE
        qt = jnp.concatenate([q] * N_HEADS, axis=0)
        qhi, qlo = _split(jnp.where(r_head == c_head, qt, 0.0))
        qhi_ref[...] = qhi
        qlo_ref[...] = qlo
        ksum_ref[...] = jnp.zeros(ksum_ref.shape, F32)
        acc_ref[...] = jnp.zeros(acc_ref.shape, F32)

    @pl.when(p < n_pages)
    def _():
        kp = kp_ref[...]
        lg_ref[p] = _dot_nt(qhi_ref[...], kp.astype(BF16))
        ksum_ref[p // pages_per_block] += jnp.sum(kp, axis=0, keepdims=True)

    @pl.when(p == n_pages - 1)
    def _():
        kmean = jnp.concatenate([ksum_ref[n] for n in range(n_blocks)], axis=0) * (1.0 / MOBA_BLOCK)
        km_hi, km_lo = _split(kmean)
        qhi = qhi_ref[...]
        s = _dot_nt(qhi, km_hi) + _dot_nt(qlo_ref[...], km_hi) + _dot_nt(qhi, km_lo)
        lane_n = lax.broadcasted_iota(jnp.int32, (rows, n_blocks), 1)
        sel = _rank_select(s, lane_n, n_blocks, MOBA_TOPK)
        knew = jnp.concatenate([new_ref[:, ATTN_DIM:2 * ATTN_DIM], pad], axis=0)
        lg_ref[n_pages] = _dot_nt(qhi, knew.astype(BF16))
        mrun = jnp.full((rows, PAGE_SIZE), NEG, F32)
        for pg in range(n_pages + 1):
            lgt = lg_ref[pg] + sbias_ref[pg]
            if pg < n_pages:
                n = pg // pages_per_block
                lgt = jnp.where(sel[:, n:n + 1] > 0.5, lgt, NEG)
            lg_ref[pg] = lgt
            mrun = jnp.maximum(mrun, lgt)
        m = jnp.max(mrun, axis=1, keepdims=True)
        srun = jnp.zeros((rows, PAGE_SIZE), F32)
        for pg in range(n_pages + 1):
            e = jnp.exp(lg_ref[pg] - m)
            srun = srun + e
            p_ref[pg] = e.astype(BF16)
        l = jnp.sum(srun, axis=1, keepdims=True)
        linv_ref[...] = jnp.broadcast_to(1.0 / l, linv_ref.shape)

    @pl.when(p >= n_pages)
    def _():
        acc_ref[...] += _dot(p_ref[p - n_pages], vp_ref[...].astype(BF16))

    @pl.when(p == 2 * n_pages - 1)
    def _():
        vnew = jnp.concatenate([new_ref[:, 2 * ATTN_DIM:3 * ATTN_DIM], pad], axis=0)
        acc = acc_ref[...] + _dot(p_ref[n_pages], vnew.astype(BF16))
        acc = jnp.where(r_head == c_head, acc * linv_ref[:, 0:1], 0.0)
        out = acc[0:t_new, :]
        for h in range(1, N_HEADS):
            out = out + acc[h * t_new:(h + 1) * t_new, :]
        o_ref[...] = out


def _moba_sample(qkv, pool_k, pool_v, page_table, sbias, layer, n_pages):
    db, t, _ = qkv.shape
    rows = N_HEADS * t
    last = n_pages - 1
    grid_spec = pltpu.PrefetchScalarGridSpec(
        num_scalar_prefetch=1,
        grid=(db, 2 * n_pages),
        in_specs=[
            pl.BlockSpec((None, t, QKV_DIM), lambda b, p, pt: (b, 0, 0)),
            pl.BlockSpec((None, None, PAGE_SIZE, ATTN_DIM),
                         lambda b, p, pt: (layer, pt[b, jnp.minimum(p, last)], 0, 0)),
            pl.BlockSpec((None, None, PAGE_SIZE, ATTN_DIM),
                         lambda b, p, pt: (layer, pt[b, jnp.maximum(p - n_pages, 0)], 0, 0)),
            pl.BlockSpec((n_pages + 1, rows, PAGE_SIZE), lambda b, p, pt: (0, 0, 0)),
        ],
        out_specs=pl.BlockSpec((None, t, ATTN_DIM), lambda b, p, pt: (b, 0, 0)),
        scratch_shapes=[
            pltpu.VMEM((rows, ATTN_DIM), BF16),
            pltpu.VMEM((rows, ATTN_DIM), BF16),
            pltpu.VMEM((n_pages + 1, rows, PAGE_SIZE), F32),
            pltpu.VMEM((n_pages * PAGE_SIZE // MOBA_BLOCK, 1, ATTN_DIM), F32),
            pltpu.VMEM((n_pages + 1, rows, PAGE_SIZE), BF16),
            pltpu.VMEM((rows, ATTN_DIM), F32),
            pltpu.VMEM((rows, 128), F32),
        ],
    )
    return pl.pallas_call(
        functools.partial(_moba_sample_kernel, n_pages=n_pages, t_new=t),
        grid_spec=grid_spec,
        out_shape=jax.ShapeDtypeStruct((db, t, ATTN_DIM), F32),
        compiler_params=_cparams(("parallel", "arbitrary")),
        name="moba_sample",
    )(page_table, qkv, pool_k, pool_v, sbias)


def _moba_sample_bias(rel_bias, past_len, t):
    n_pages = past_len // PAGE_SIZE
    rows = N_HEADS * t
    pg = np.arange(n_pages + 1)[:, None, None]
    row = np.arange(rows)[None, :, None]
    j = np.arange(PAGE_SIZE)[None, None, :]
    shape = (n_pages + 1, rows, PAGE_SIZE)
    h = np.broadcast_to(row // t, shape)
    tq = row % t
    is_new = pg == n_pages
    dist = np.broadcast_to(np.where(is_new, tq - j, past_len + tq - (pg * PAGE_SIZE + j)), shape)
    valid = np.broadcast_to(np.where(is_new, (j <= tq) & (j < t), True), shape)
    return _bias_table(rel_bias, h, dist, valid)


def _dil_prompt_kernel(q_ref, kp_ref, kc_ref, vp_ref, vc_ref, bias_ref, o_ref, lse_ref):
    c = pl.program_id(2)
    ch = DIL_CHUNK
    col = lax.broadcasted_iota(jnp.int32, (ch, 2 * ch), 1)
    keep = (col >= ch) | (c > 0)
    for h in range(GROUP_HEADS):
        sl = slice(h * HEAD_DIM, (h + 1) * HEAD_DIM)
        q = (q_ref[:, sl] * ATTN_SCALE).astype(BF16)
        k = jnp.concatenate([kp_ref[:, sl], kc_ref[:, sl]], axis=0).astype(BF16)
        v = jnp.concatenate([vp_ref[:, sl], vc_ref[:, sl]], axis=0).astype(BF16)
        s = jnp.where(keep, _dot_nt(q, k) + bias_ref[h], NEG)
        m = jnp.max(s, axis=1, keepdims=True)
        p = jnp.exp(s - m)
        l = jnp.sum(p, axis=1, keepdims=True)
        o_ref[:, sl] = _dot(p.astype(BF16), v) / l
        lse_ref[:, sl] = jnp.broadcast_to(m + jnp.log(l), (ch, HEAD_DIM))


def _dil_prompt(qkv, bias, g, dil):
    b, s, _ = qkv.shape
    ch = DIL_CHUNK
    sub = s // dil
    nch = sub // ch
    nblk = QKV_DIM // GROUP_DIM
    view = qkv.reshape(b, sub, dil * QKV_DIM)
    kcol = ATTN_DIM // GROUP_DIM + g
    vcol = 2 * ATTN_DIM // GROUP_DIM + g
    spec = lambda colblk, prev: pl.BlockSpec(
        (None, ch, GROUP_DIM),
        (lambda bb, r, c: (bb, jnp.maximum(c - 1, 0), r * nblk + colblk)) if prev
        else (lambda bb, r, c: (bb, c, r * nblk + colblk)))
    out_spec = pl.BlockSpec((None, ch, GROUP_DIM), lambda bb, r, c: (bb, c, r))
    o, lse = pl.pallas_call(
        _dil_prompt_kernel,
        grid=(b, dil, nch),
        in_specs=[spec(g, False), spec(kcol, True), spec(kcol, False), spec(vcol, True), spec(vcol, False),
                  pl.BlockSpec((GROUP_HEADS, ch, 2 * ch), lambda bb, r, c: (0, 0, 0))],
        out_specs=[out_spec, out_spec],
        out_shape=[jax.ShapeDtypeStruct((b, sub, dil * GROUP_DIM), F32)] * 2,
        compiler_params=_cparams(("parallel", "parallel", "arbitrary")),
        name=f"dil_prompt{g}",
    )(view, view, view, view, view, bias)
    return o.reshape(b, s, GROUP_DIM), lse.reshape(b, s, GROUP_DIM)


def _dil_prompt_bias(rel_bias, g, win, dil):
    ch = DIL_CHUNK
    shape = (GROUP_HEADS, ch, 2 * ch)
    h = np.broadcast_to(g * GROUP_HEADS + np.arange(GROUP_HEADS)[:, None, None], shape)
    i = np.arange(ch)[None, :, None]
    ik = np.arange(2 * ch)[None, None, :]
    steps = np.broadcast_to(i + ch - ik, shape)
    valid = (steps >= 0) & (steps <= win // dil)
    return _bias_table(rel_bias, h, steps * dil, valid)


def _dil_sample_kernel(new_ref, buf_ref, bb_ref, bn_ref, o_ref, lse_ref, *, g, dil, t_new):
    hp = 8
    r_head = lax.broadcasted_iota(jnp.int32, (hp, GROUP_DIM), 0)
    c_head = lax.broadcasted_iota(jnp.int32, (hp, GROUP_DIM), 1) // HEAD_DIM
    diag = r_head == c_head
    pad = jnp.zeros((DIL_CHUNK - t_new, GROUP_DIM), F32)
    qcol = g * GROUP_DIM
    kcol = ATTN_DIM + g * GROUP_DIM
    vcol = 2 * ATTN_DIM + g * GROUP_DIM
    knew = jnp.concatenate([new_ref[:, kcol:kcol + GROUP_DIM], pad], axis=0).astype(BF16)
    vnew = jnp.concatenate([new_ref[:, vcol:vcol + GROUP_DIM], pad], axis=0).astype(BF16)
    outs, lses = [], []
    for t in range(t_new):
        base = (t % dil) * 2 * GROUP_DIM
        kb = buf_ref[:, base:base + GROUP_DIM].astype(BF16)
        vb = buf_ref[:, base + GROUP_DIM:base + 2 * GROUP_DIM].astype(BF16)
        q = new_ref[t:t + 1, qcol:qcol + GROUP_DIM] * ATTN_SCALE
        qbd = jnp.where(diag, jnp.broadcast_to(q, (hp, GROUP_DIM)), 0.0).astype(BF16)
        lb = _dot_nt(qbd, kb) + bb_ref[t]
        ln = _dot_nt(qbd, knew) + bn_ref[t]
        m = jnp.maximum(jnp.max(lb, axis=1, keepdims=True), jnp.max(ln, axis=1, keepdims=True))
        pb = jnp.exp(lb - m)
        pn = jnp.exp(ln - m)
        l = jnp.sum(pb, axis=1, keepdims=True) + jnp.sum(pn, axis=1, keepdims=True)
        o8 = (_dot(pb.astype(BF16), vb) + _dot(pn.astype(BF16), vnew)) / l
        outs.append(jnp.sum(jnp.where(diag, o8, 0.0), axis=0, keepdims=True))
        lses.append(jnp.sum(jnp.where(diag, m + jnp.log(l), 0.0), axis=0, keepdims=True))
    o_ref[...] = jnp.concatenate(outs, axis=0)
    lse_ref[...] = jnp.concatenate(lses, axis=0)


def _dil_sample(qkv, state, bb, bn, layer, g, dil):
    db, t, _ = qkv.shape
    nb, _, lb = state.shape[:3]
    rows = lb // dil
    ncol = min(dil, t)
    view = state.reshape(nb, db, rows, dil * 2 * GROUP_DIM)
    out_spec = pl.BlockSpec((None, t, GROUP_DIM), lambda b: (b, 0, 0))
    return pl.pallas_call(
        functools.partial(_dil_sample_kernel, g=g, dil=dil, t_new=t),
        grid=(db,),
        in_specs=[
            pl.BlockSpec((None, t, QKV_DIM), lambda b: (b, 0, 0)),
            pl.BlockSpec((None, None, rows, ncol * 2 * GROUP_DIM), lambda b: (layer, b, 0, 0)),
            pl.BlockSpec((t, 8, rows), lambda b: (0, 0, 0)),
            pl.BlockSpec((t, 8, DIL_CHUNK), lambda b: (0, 0, 0)),
        ],
        out_specs=[out_spec, out_spec],
        out_shape=[jax.ShapeDtypeStruct((db, t, GROUP_DIM), F32)] * 2,
        compiler_params=_cparams(("parallel",)),
        name=f"dil_sample{g}",
    )(qkv, view, bb, bn)


def _dil_sample_bias(rel_bias, g, win, dil, lb, t_new):
    rows = lb // dil
    hp = 8
    shape = (t_new, hp, rows)
    hrow = np.arange(hp)[None, :, None]
    h = np.broadcast_to(g * GROUP_HEADS + np.minimum(hrow, GROUP_HEADS - 1), shape)
    t = np.arange(t_new)[:, None, None]
    m = np.arange(rows)[None, None, :]
    dist = np.broadcast_to(lb + t - m * dil - (t % dil), shape)
    valid = np.broadcast_to((dist <= win) | (hrow >= GROUP_HEADS), shape)
    bb = _bias_table(rel_bias, h, dist, valid)
    shape = (t_new, hp, DIL_CHUNK)
    h = np.broadcast_to(g * GROUP_HEADS + np.minimum(hrow, GROUP_HEADS - 1), shape)
    tk = np.arange(DIL_CHUNK)[None, None, :]
    dist = np.broadcast_to(t - tk, shape)
    valid = np.broadcast_to((tk < t_new) & (dist >= 0) & (dist % dil == 0) & (dist <= win), shape)
    bn = _bias_table(rel_bias, h, dist, valid)
    return bb, bn


def _oproj_kernel(h_ref, o_ref, gate_ref, whi_ref, wlo_ref, out_ref):
    out_ref[...] = h_ref[...] + gate_ref[...] * _dot3(o_ref[...], whi_ref[...], wlo_ref[...])


def _oproj_dil_kernel(h_ref, o0_ref, o1_ref, o2_ref, l0_ref, l1_ref, l2_ref, gate_ref, whi_ref, wlo_ref, out_ref):
    l0, l1, l2 = l0_ref[...], l1_ref[...], l2_ref[...]
    mx = jnp.maximum(jnp.maximum(l0, l1), l2)
    e0, e1, e2 = jnp.exp(l0 - mx), jnp.exp(l1 - mx), jnp.exp(l2 - mx)
    inv = 1.0 / (e0 + e1 + e2)
    o = jnp.concatenate([o0_ref[...] * (e0 * inv), o1_ref[...] * (e1 * inv), o2_ref[...] * (e2 * inv)], axis=1)
    out_ref[...] = h_ref[...] + gate_ref[...] * _dot3(o, whi_ref[...], wlo_ref[...])


def _oproj(h, attn, gate, w_hi, w_lo, tm, tiles_per_seg):
    n, d = h.shape
    row = lambda w: pl.BlockSpec((tm, w), lambda i: (i, 0))
    if isinstance(attn, tuple):
        kern, attn_args = _oproj_dil_kernel, list(attn[0]) + list(attn[1])
        attn_specs = [row(GROUP_DIM)] * 6
    else:
        kern, attn_args, attn_specs = _oproj_kernel, [attn], [row(ATTN_DIM)]
    return pl.pallas_call(
        kern,
        grid=(n // tm,),
        in_specs=[row(d)] + attn_specs + [
            _mod_spec(gate, tm, tiles_per_seg),
            pl.BlockSpec((ATTN_DIM, d), lambda i: (0, 0)),
            pl.BlockSpec((ATTN_DIM, d), lambda i: (0, 0)),
        ],
        out_specs=row(d),
        out_shape=jax.ShapeDtypeStruct((n, d), F32),
        compiler_params=_cparams(("parallel",)),
        name="oproj",
    )(h, *attn_args, gate, w_hi, w_lo)


def _topk_rows(src_ref, nrows, k, val_ref, idx_ref):
    ncol = src_ref.shape[1]
    ridx = lax.broadcasted_iota(jnp.int32, (nrows, ncol), 0).astype(F32)

    def body(it, carry):
        a = src_ref[0:nrows, :]
        mx = jnp.max(a, axis=0, keepdims=True)
        ix = jnp.min(jnp.where(a == mx, ridx, float(nrows)), axis=0, keepdims=True)
        src_ref[0:nrows, :] = jnp.where(ridx == ix, -jnp.inf, a)
        val_ref[pl.ds(it, 1), :] = mx
        idx_ref[pl.ds(it, 1), :] = ix
        return carry

    lax.fori_loop(0, k, body, 0)


def _peer_select_kernel(x_ref, g_ref, sh_ref, sc_ref, whi_ref, wlo_ref, sk_ref, a_ref, b_ref, gate_ref,
                        s_ref, t1_ref, i1_ref, t2_ref, i2_ref, tc_ref, ic_ref):
    k = PEER_TOPK
    half = PEER_DK // 2
    f = _norm_mod(x_ref[...], g_ref[...], sh_ref[...], sc_ref[...])
    q = _dot3(f, whi_ref[...], wlo_ref[...])
    s_ref[0:PEER_NKEYS, :] = _dot3_nt(sk_ref[0], q[:, 0:half])
    _topk_rows(s_ref, PEER_NKEYS, k, t1_ref, i1_ref)
    s_ref[0:PEER_NKEYS, :] = _dot3_nt(sk_ref[1], q[:, half:2 * half])
    _topk_rows(s_ref, PEER_NKEYS, k, t2_ref, i2_ref)
    t2 = t2_ref[...]
    for a in range(k):
        s_ref[a * k:(a + 1) * k, :] = t1_ref[a:a + 1, :] + t2
    _topk_rows(s_ref, k * k, k, tc_ref, ic_ref)
    jc = ic_ref[...]
    ja = jnp.floor(jc * (1.0 / k))
    jb = jc - ja * k
    e1 = jnp.zeros(jc.shape, F32)
    e2 = jnp.zeros(jc.shape, F32)
    for a in range(k):
        e1 = e1 + jnp.where(ja == float(a), i1_ref[a:a + 1, :], 0.0)
        e2 = e2 + jnp.where(jb == float(a), i2_ref[a:a + 1, :], 0.0)
    top = tc_ref[...]
    e = jnp.exp(top - top[0:1, :])
    a_ref[...] = e1
    b_ref[...] = e2
    gate_ref[...] = e / jnp.sum(e, axis=0, keepdims=True)


def _peer_select(x, g, shift, scale, wq_hi, wq_lo, sub_keys, tm, tiles_per_seg):
    n, d = x.shape
    k = PEER_TOPK
    out_spec = pl.BlockSpec((k, tm), lambda i, h: (h, i))
    out_shape = jax.ShapeDtypeStruct((PEER_HEADS * k, n), F32)
    mod_spec = lambda mod: pl.BlockSpec((None, mod.shape[1], d), lambda i, h: (i // tiles_per_seg, 0, 0))
    return pl.pallas_call(
        _peer_select_kernel,
        grid=(n // tm, PEER_HEADS),
        in_specs=[
            pl.BlockSpec((tm, d), lambda i, h: (i, 0)),
            pl.BlockSpec((1, d), lambda i, h: (0, 0)),
            mod_spec(shift),
            mod_spec(scale),
            pl.BlockSpec((d, PEER_DK), lambda i, h: (0, h)),
            pl.BlockSpec((d, PEER_DK), lambda i, h: (0, h)),
            pl.BlockSpec((2, PEER_NKEYS, PEER_DK // 2), lambda i, h: (0, 0, 0)),
        ],
        out_specs=[out_spec] * 3,
        out_shape=[out_shape] * 3,
        scratch_shapes=[pltpu.VMEM((k * k, tm), F32)] + [pltpu.VMEM((k, tm), F32)] * 6,
        compiler_params=_cparams(("parallel", "arbitrary")),
        name="peer_select",
    )(x, g.reshape(1, d), shift, scale, wq_hi, wq_lo, sub_keys)


def _peer_dense_kernel(x_ref, g_ref, sh_ref, sc_ref, gt_ref, a_ref, b_ref, w_ref, u_ref, v_ref, out_ref,
                       f_ref, gs_ref, at_ref, bt_ref, wt_ref, p_ref, acc_ref):
    e = pl.program_id(1)
    tc = x_ref.shape[0]
    te = u_ref.shape[0]
    nk = PEER_NKEYS

    @pl.when(e == 0)
    def _():
        f_ref[...] = _norm_mod(x_ref[...], g_ref[...], sh_ref[...], sc_ref[...]).astype(BF16)
        acc_ref[...] = jnp.zeros(acc_ref.shape, F32)
        at_ref[...] = a_ref[...].T
        bt_ref[...] = b_ref[...].T
        wt_ref[...] = w_ref[...].T
        key = lax.broadcasted_iota(jnp.int32, (nk, nk), 0).astype(F32)

        def build(c, carry):
            first = jnp.where(key == at_ref[pl.ds(c, 1), :], 1.0, 0.0).astype(BF16)
            second = jnp.where(key == bt_ref[pl.ds(c, 1), :], wt_ref[pl.ds(c, 1), :], 0.0).astype(BF16)
            gs_ref[pl.ds(pl.multiple_of(c * nk, nk), nk), :] = _dot_nt(first, second)
            return carry

        lax.fori_loop(0, tc, build, 0)

    act = _dot_nt(f_ref[...], u_ref[...])
    for r in range(te // nk):
        gates = gs_ref[pl.ds(e * (te // nk) + r, tc, stride=nk), :]
        a = act[:, r * nk:(r + 1) * nk]
        gelu = 0.5 * a * (1.0 + lax.erf(a * (2.0 ** -0.5)))
        p_ref[:, r * nk:(r + 1) * nk] = (gates * gelu).astype(BF16)
    acc_ref[...] += _dot(p_ref[...], v_ref[...])

    @pl.when(e == pl.num_programs(1) - 1)
    def _():
        out_ref[...] = x_ref[...] + gt_ref[...] * acc_ref[...]


def _peer_dense(x, g, shift, scale, gate, first, second, weight, u_tab, v_tab, tc, te, tiles_per_seg):
    n, d = x.shape
    n_exp = u_tab.shape[0]
    nhk = PEER_HEADS * PEER_TOPK
    mod_spec = lambda mod: pl.BlockSpec((None, mod.shape[1], d), lambda i, e: (i // tiles_per_seg, 0, 0))
    sel_spec = pl.BlockSpec((nhk, tc), lambda i, e: (0, i))
    return pl.pallas_call(
        _peer_dense_kernel,
        grid=(n // tc, n_exp // te),
        in_specs=[
            pl.BlockSpec((tc, d), lambda i, e: (i, 0)),
            pl.BlockSpec((1, d), lambda i, e: (0, 0)),
            mod_spec(shift), mod_spec(scale), mod_spec(gate),
            sel_spec, sel_spec, sel_spec,
            pl.BlockSpec((te, d), lambda i, e: (e, 0)),
            pl.BlockSpec((te, d), lambda i, e: (e, 0)),
        ],
        out_specs=pl.BlockSpec((tc, d), lambda i, e: (i, 0)),
        out_shape=jax.ShapeDtypeStruct((n, d), F32),
        scratch_shapes=[
            pltpu.VMEM((tc, d), BF16),
            pltpu.VMEM((tc * PEER_NKEYS, PEER_NKEYS), F32),
            pltpu.VMEM((tc, nhk), F32),
            pltpu.VMEM((tc, nhk), F32),
            pltpu.VMEM((tc, nhk), F32),
            pltpu.VMEM((tc, te), BF16),
            pltpu.VMEM((tc, d), F32),
        ],
        compiler_params=_cparams(("parallel", "arbitrary")),
        name="peer_dense",
    )(x, g.reshape(1, d), shift, scale, gate, first, second, weight, u_tab, v_tab)


def _final_norm_kernel(x_ref, g_ref, o_ref):
    x = x_ref[...]
    ms = jnp.mean(x * x, axis=-1, keepdims=True)
    o_ref[...] = x * lax.rsqrt(ms + RMS_EPS) * g_ref[...]


def _final_norm(x, g, tm):
    n, d = x.shape
    return pl.pallas_call(
        _final_norm_kernel,
        grid=(n // tm,),
        in_specs=[pl.BlockSpec((tm, d), lambda i: (i, 0)), pl.BlockSpec((1, d), lambda i: (0, 0))],
        out_specs=pl.BlockSpec((tm, d), lambda i: (i, 0)),
        out_shape=jax.ShapeDtypeStruct((n, d), F32),
        compiler_params=_cparams(("parallel",)),
        name="final_norm",
    )(x, g.reshape(1, d))


TM = 256
PEER_TC = 256
PEER_TE = 2048


def _split_w(w):
    hi = w.astype(BF16)
    return hi, (w - hi.astype(F32)).astype(BF16)


def kernel(x_prompt, x_sample, cache_k_moba, cache_v_moba, state_kv_dil0, state_kv_dil1, state_kv_dil2, page_table,
           c_prompt, c_sample, w_ada, b_ada, g_norm, w_qkv, w_o, rel_bias, w_pq, peer_sub_keys, peer_u, peer_v,
           g_final):
    B, S, D = x_prompt.shape
    DB, T, _ = x_sample.shape
    n_a, n_pool = cache_k_moba.shape[:2]
    n_pages = page_table.shape[1]
    past_len = n_pages * PAGE_SIZE
    assert D == D_MODEL and S % MOBA_BLOCK == 0 and past_len % MOBA_BLOCK == 0 and S % TM == 0
    dil_states = (state_kv_dil0, state_kv_dil1, state_kv_dil2)
    for st, (win, dil) in zip(dil_states, DIL_GROUPS):
        assert st.shape[2] == win and win // dil == DIL_CHUNK and S % (dil * DIL_CHUNK) == 0

    np_, ns_ = B * S, DB * T
    tms = min(TM, ns_)
    hp = x_prompt.reshape(np_, D)
    hs = x_sample.reshape(ns_, D)

    mod_all = _ada(jnp.concatenate([c_prompt, c_sample], axis=0), w_ada, b_ada)
    wqkv_hi, wqkv_lo = _split_w(w_qkv)
    wo_hi, wo_lo = _split_w(w_o)
    wpq_hi, wpq_lo = _split_w(w_pq)
    u_bf, v_bf = peer_u.astype(BF16), peer_v.astype(BF16)
    pool_k = cache_k_moba.reshape(n_a, n_pool, PAGE_SIZE, ATTN_DIM)
    pool_v = cache_v_moba.reshape(n_a, n_pool, PAGE_SIZE, ATTN_DIM)

    moba_bias_p = _moba_prompt_bias(rel_bias, S)
    moba_bias_s = _moba_sample_bias(rel_bias, past_len, T)
    dil_bias_p = [_dil_prompt_bias(rel_bias, g, w, d) for g, (w, d) in enumerate(DIL_GROUPS)]
    dil_bias_s = [_dil_sample_bias(rel_bias, g, w, d, dil_states[g].shape[2], T)
                  for g, (w, d) in enumerate(DIL_GROUPS)]

    kp_rows, vp_rows, ks_rows, vs_rows = [], [], [], []
    dil_p = [[] for _ in DIL_GROUPS]
    dil_s = [[] for _ in DIL_GROUPS]
    for i in range(DEPTH):
        j = i // 2
        mod = mod_all[i].reshape(B + DB, 6, D)
        mp = [mod[:B, c].reshape(B, 1, D) for c in range(6)]
        ms = [jnp.repeat(mod[B:, c], T, axis=0).reshape(ns_ // tms, tms, D) for c in range(6)]
        seg_p, seg_s = S // TM, 1

        qkv_p = _nmm(hp, g_norm[i, 0], mp[0], mp[1], wqkv_hi[i], wqkv_lo[i], TM, seg_p)
        qkv_s = _nmm(hs, g_norm[i, 0], ms[0], ms[1], wqkv_hi[i], wqkv_lo[i], tms, seg_s)
        qkv_p3 = qkv_p.reshape(B, S, QKV_DIM)
        qkv_s3 = qkv_s.reshape(DB, T, QKV_DIM)
        heads = lambda a, n, l: a.reshape(n, l, 3, N_HEADS, HEAD_DIM)
        qkv_p5, qkv_s5 = heads(qkv_p, B, S), heads(qkv_s, DB, T)
        if i % 2 == 0:
            attn_p = _moba_prompt(qkv_p3, moba_bias_p).reshape(np_, ATTN_DIM)
            attn_s = _moba_sample(qkv_s3, pool_k, pool_v, page_table, moba_bias_s, j, n_pages).reshape(ns_, ATTN_DIM)
            kp_rows.append(qkv_p5[:, :, 1])
            vp_rows.append(qkv_p5[:, :, 2])
            ks_rows.append(qkv_s5[:, :, 1])
            vs_rows.append(qkv_s5[:, :, 2])
        else:
            op, lp, os_, ls = [], [], [], []
            for g, (win, dil) in enumerate(DIL_GROUPS):
                o, lse = _dil_prompt(qkv_p3, dil_bias_p[g], g, dil)
                op.append(o.reshape(np_, GROUP_DIM))
                lp.append(lse.reshape(np_, GROUP_DIM))
                o, lse = _dil_sample(qkv_s3, dil_states[g], *dil_bias_s[g], j, g, dil)
                os_.append(o.reshape(ns_, GROUP_DIM))
                ls.append(lse.reshape(ns_, GROUP_DIM))
                hsl = slice(g * GROUP_HEADS, (g + 1) * GROUP_HEADS)
                kv_p = jnp.stack([qkv_p5[:, :, 1, hsl], qkv_p5[:, :, 2, hsl]], axis=2)
                dil_p[g].append(kv_p[:, S - min(win, S):])
                kv_s = jnp.stack([qkv_s5[:, :, 1, hsl], qkv_s5[:, :, 2, hsl]], axis=2)
                kv_s = jnp.concatenate([dil_states[g][j], kv_s], axis=1)
                dil_s[g].append(kv_s[:, kv_s.shape[1] - min(win, past_len + T):])
            attn_p, attn_s = (op, lp), (os_, ls)
        hp = _oproj(hp, attn_p, mp[2], wo_hi[i], wo_lo[i], TM, seg_p)
        hs = _oproj(hs, attn_s, ms[2], wo_hi[i], wo_lo[i], tms, seg_s)

        sel_p = _peer_select(hp, g_norm[i, 1], mp[3], mp[4], wpq_hi[i], wpq_lo[i], peer_sub_keys[i], TM, seg_p)
        sel_s = _peer_select(hs, g_norm[i, 1], ms[3], ms[4], wpq_hi[i], wpq_lo[i], peer_sub_keys[i], tms, seg_s)
        hp = _peer_dense(hp, g_norm[i, 1], mp[3], mp[4], mp[5], *sel_p, u_bf[i], v_bf[i], PEER_TC, PEER_TE,
                         S // PEER_TC)
        hs = _peer_dense(hs, g_norm[i, 1], ms[3], ms[4], ms[5], *sel_s, u_bf[i], v_bf[i], tms, PEER_TE, 1)

    y_prompt = _final_norm(hp, g_final, TM).reshape(B, S, D)
    y_sample = _final_norm(hs, g_final, tms).reshape(DB, T, D)
    return (y_prompt, y_sample,
            jnp.stack(kp_rows), jnp.stack(vp_rows), jnp.stack(ks_rows), jnp.stack(vs_rows),
            jnp.stack(dil_p[0]), jnp.stack(dil_p[1]), jnp.stack(dil_p[2]),
            jnp.stack(dil_s[0]), jnp.stack(dil_s[1]), jnp.stack(dil_s[2]))
```

```python
import functools
import math

import numpy as np
import jax
import jax.numpy as jnp
from jax import lax
from jax.experimental import pallas as pl
from jax.experimental.pallas import tpu as pltpu

F32 = jnp.float32
BF16 = jnp.bfloat16
NEG = -1e30

D_MODEL = 1024
N_HEADS = 12
HEAD_DIM = 64
ATTN_DIM = N_HEADS * HEAD_DIM
QKV_DIM = 3 * ATTN_DIM
ATTN_SCALE = HEAD_DIM ** -0.5
DEPTH = 4
PAGE_SIZE = 128
MOBA_BLOCK = 256
MOBA_TOPK = 3
DIL_GROUPS = ((128, 1), (512, 4), (2048, 16))
GROUP_HEADS = 4
GROUP_DIM = GROUP_HEADS * HEAD_DIM
DIL_CHUNK = 128
REL_BUCKETS = 32
REL_MAX_DIST = 2048
PEER_HEADS = 8
PEER_NKEYS = 128
PEER_DK = 256
PEER_TOPK = 16
RMS_EPS = 1e-6

VMEM_LIMIT = 56 * 1024 * 1024


def _cparams(sem):
    return pltpu.CompilerParams(dimension_semantics=sem, vmem_limit_bytes=VMEM_LIMIT)


def _split(x):
    hi = x.astype(BF16)
    lo = (x - hi.astype(F32)).astype(BF16)
    return hi, lo


def _dot(a, b):
    return jnp.dot(a, b, preferred_element_type=F32)


def _dot_nt(a, b):
    return lax.dot_general(a, b, (((1,), (1,)), ((), ())), preferred_element_type=F32)


def _dot3(a, b_hi, b_lo):
    a_hi, a_lo = _split(a)
    return _dot(a_hi, b_hi) + _dot(a_lo, b_hi) + _dot(a_hi, b_lo)


def _dot3_nt(a, b):
    a_hi, a_lo = _split(a)
    b_hi, b_lo = _split(b)
    return _dot_nt(a_hi, b_hi) + _dot_nt(a_lo, b_hi) + _dot_nt(a_hi, b_lo)


def _norm_mod(x, g, shift, scale):
    ms = jnp.mean(x * x, axis=-1, keepdims=True)
    xn = x * lax.rsqrt(ms + RMS_EPS) * g
    return xn * (1.0 + scale) + shift


def _bucket_np(dist):
    n = np.maximum(dist, 0)
    max_exact = REL_BUCKETS // 2
    nf = np.maximum(n, 1).astype(np.float64)
    large = max_exact + (np.log(nf / max_exact) / math.log(REL_MAX_DIST / max_exact)
                         * (REL_BUCKETS - max_exact)).astype(np.int64)
    large = np.minimum(large, REL_BUCKETS - 1)
    return np.where(n < max_exact, n, large).astype(np.int32)


BIAS_PAD = 256
BIAS_LEN = 4352


def _bias_by_distance(rel_bias):
    onehot = np.zeros((BIAS_LEN, REL_BUCKETS), np.float32)
    onehot[np.arange(BIAS_LEN), _bucket_np(np.arange(BIAS_LEN))] = 1.0
    by_dist = jnp.dot(jnp.asarray(onehot), rel_bias, precision=lax.Precision.HIGHEST)
    return jnp.concatenate([jnp.full((N_HEADS, BIAS_PAD), NEG, F32), by_dist.T], axis=1)


def _toeplitz(wp, rows, cols):
    n2 = wp.shape[-1]
    x = jnp.tile(wp, (1,) * (wp.ndim - 1) + (rows,))[..., :rows * (n2 - 1)]
    return x.reshape(wp.shape[:-1] + (rows, n2 - 1))[..., :cols]


def _rank_select(s, lane_n, n_valid, topk):
    nb = s.shape[1]
    rank = jnp.zeros(s.shape, F32)
    for m in range(nb):
        sm = s[:, m:m + 1]
        beats = (sm > s) | ((sm == s) & (m < lane_n))
        beats = beats & (m < n_valid)
        rank = rank + jnp.where(beats, 1.0, 0.0)
    sel = (lane_n < n_valid) & (rank < float(topk))
    return jnp.where(sel, 1.0, 0.0)


def _rank_select_rows(s, row_n, n_valid, topk):
    nb = s.shape[0]
    rank = jnp.zeros(s.shape, F32)
    for m in range(nb):
        sm = s[m:m + 1, :]
        beats = (sm > s) | ((sm == s) & (m < row_n))
        beats = beats & (m < n_valid)
        rank = rank + jnp.where(beats, 1.0, 0.0)
    sel = (row_n < n_valid) & (rank < float(topk))
    return jnp.where(sel, 1.0, 0.0)


def _ada_kernel(c_ref, w_ref, b_ref, o_ref):
    c = c_ref[...]
    s = c * (1.0 / (1.0 + jnp.exp(-c)))
    w_hi, w_lo = _split(w_ref[...])
    o_ref[...] = _dot3(s, w_hi, w_lo) + b_ref[...]


def _ada(c_all, w_ada, b_ada):
    n = c_all.shape[0]
    depth, d, n6 = w_ada.shape
    tn = 1536
    return pl.pallas_call(
        _ada_kernel,
        grid=(depth, n6 // tn),
        in_specs=[
            pl.BlockSpec((n, d), lambda i, j: (0, 0)),
            pl.BlockSpec((None, d, tn), lambda i, j: (i, 0, j)),
            pl.BlockSpec((None, 1, tn), lambda i, j: (i, 0, j)),
        ],
        out_specs=pl.BlockSpec((None, n, tn), lambda i, j: (i, 0, j)),
        out_shape=jax.ShapeDtypeStruct((depth, n, n6), F32),
        compiler_params=_cparams(("parallel", "parallel")),
        name="ada",
    )(c_all, w_ada, b_ada.reshape(depth, 1, n6))


def _nmm_kernel(x_ref, g_ref, sh_ref, sc_ref, whi_ref, wlo_ref, o_ref):
    f = _norm_mod(x_ref[...], g_ref[...], sh_ref[...], sc_ref[...])
    o_ref[...] = _dot3(f, whi_ref[...], wlo_ref[...])


def _mod_spec(mod, tm, tiles_per_seg):
    r = mod.shape[1]
    return pl.BlockSpec((None, r, D_MODEL), lambda i, *_: (i // tiles_per_seg, 0, 0))


def _nmm(x, g, shift, scale, w_hi, w_lo, tm, tiles_per_seg):
    n, d = x.shape
    nout = w_hi.shape[1]
    return pl.pallas_call(
        _nmm_kernel,
        grid=(n // tm,),
        in_specs=[
            pl.BlockSpec((tm, d), lambda i: (i, 0)),
            pl.BlockSpec((1, d), lambda i: (0, 0)),
            _mod_spec(shift, tm, tiles_per_seg),
            _mod_spec(scale, tm, tiles_per_seg),
            pl.BlockSpec((d, nout), lambda i: (0, 0)),
            pl.BlockSpec((d, nout), lambda i: (0, 0)),
        ],
        out_specs=pl.BlockSpec((tm, nout), lambda i: (i, 0)),
        out_shape=jax.ShapeDtypeStruct((n, nout), F32),
        compiler_params=_cparams(("parallel",)),
        name="nmm",
    )(x, g.reshape(1, d), shift, scale, w_hi, w_lo)


def _moba_prompt_kernel(q_ref, k_ref, v_ref, bias_ref, o_ref, kmean_ref, mask_ref):
    i = pl.program_id(2)
    blk = MOBA_BLOCK
    nb = k_ref.shape[0] // blk

    @pl.when(i == 0)
    def _():
        for n in range(nb):
            kmean_ref[n:n + 1, :] = jnp.mean(k_ref[n * blk:(n + 1) * blk, :], axis=0, keepdims=True)

    lane_head = lax.broadcasted_iota(jnp.int32, (blk, 128), 1) // HEAD_DIM
    row_n = lax.broadcasted_iota(jnp.int32, (nb, blk), 0)
    km_head = lax.broadcasted_iota(jnp.int32, (nb, 128), 1) // HEAD_DIM
    q2 = q_ref[...] * ATTN_SCALE
    qb = []
    for hh in range(2):
        q = jnp.where(lane_head == hh, q2, 0.0)
        scores = _dot3_nt(jnp.where(km_head == hh, kmean_ref[...], 0.0), q)
        sel = _rank_select_rows(scores, row_n, i, MOBA_TOPK)
        sel = jnp.concatenate([sel, jnp.zeros((128 - nb, blk), F32)], axis=0).T
        for n in range(nb):
            mask_ref[hh, n] = jnp.broadcast_to(sel[:, n:n + 1], (blk, 128))
        qb.append(q.astype(BF16))

    def tile(j, hh, masked):
        r0 = pl.multiple_of(j * blk, blk)
        logits = _dot_nt(qb[hh], k_ref[pl.ds(r0, blk), :].astype(BF16)) + bias_ref[i - j, hh]
        if masked:
            keep = mask_ref[hh, j] > 0.5
            logits = jnp.where(jnp.concatenate([keep] * (blk // 128), axis=1), logits, NEG)
        return logits, v_ref[pl.ds(r0, blk), :].astype(BF16)

    state = []
    for hh in range(2):
        logits, vb = tile(i, hh, False)
        m = jnp.max(logits, axis=1, keepdims=True)
        p = jnp.exp(logits - m)
        state += [m, jnp.sum(p, axis=1, keepdims=True), _dot(p.astype(BF16), vb)]

    def body(j, carry):
        out = []
        for hh in range(2):
            m, l, acc = carry[3 * hh:3 * hh + 3]
            logits, vb = tile(j, hh, True)
            m_new = jnp.maximum(m, jnp.max(logits, axis=1, keepdims=True))
            alpha = jnp.exp(m - m_new)
            p = jnp.exp(logits - m_new)
            out += [m_new, alpha * l + jnp.sum(p, axis=1, keepdims=True), alpha * acc + _dot(p.astype(BF16), vb)]
        return tuple(out)

    _, l0, acc0, _, l1, acc1 = lax.fori_loop(0, i, body, tuple(state))
    o_ref[...] = jnp.where(lane_head == 0, acc0 / l0, acc1 / l1)


def _moba_prompt(qkv, bias):
    b, s, _ = qkv.shape
    blk = MOBA_BLOCK
    nb = s // blk
    npair = N_HEADS // 2
    return pl.pallas_call(
        _moba_prompt_kernel,
        grid=(npair, b, nb),
        in_specs=[
            pl.BlockSpec((None, blk, 128), lambda hp, bb, i: (bb, i, hp)),
            pl.BlockSpec((None, s, 128), lambda hp, bb, i: (bb, 0, npair + hp)),
            pl.BlockSpec((None, s, 128), lambda hp, bb, i: (bb, 0, 2 * npair + hp)),
            pl.BlockSpec((nb, 2, blk, blk), lambda hp, bb, i: (0, hp, 0, 0)),
        ],
        out_specs=pl.BlockSpec((None, blk, 128), lambda hp, bb, i: (bb, i, hp)),
        out_shape=jax.ShapeDtypeStruct((b, s, ATTN_DIM), F32),
        scratch_shapes=[pltpu.VMEM((nb, 128), F32), pltpu.VMEM((2, nb, blk, 128), F32)],
        compiler_params=_cparams(("parallel", "parallel", "arbitrary")),
        name="moba_prompt",
    )(qkv, qkv, qkv, bias)


def _moba_prompt_bias(bias_d, s):
    blk = MOBA_BLOCK
    nb = s // blk
    assert blk <= BIAS_PAD
    wps = []
    for delta in range(nb):
        z = BIAS_PAD + delta * blk
        lo = bias_d[:, z - blk:z + 1][:, ::-1]
        hi = bias_d[:, z + 1:z + blk][:, ::-1]
        wps.append(jnp.concatenate([lo, hi], axis=1))
    return _toeplitz(jnp.stack(wps), blk, blk)


def _moba_sample_kernel(pt_ref, new_ref, *refs, n_pages, t_new):
    del pt_ref
    kt_refs, vt_refs = refs[:n_pages], refs[n_pages:2 * n_pages]
    sbias_ref, o_ref, lg_ref, p_ref = refs[2 * n_pages:]
    rows = N_HEADS * t_new
    pages_per_block = MOBA_BLOCK // PAGE_SIZE
    n_blocks = n_pages // pages_per_block
    r_head = lax.broadcasted_iota(jnp.int32, (rows, ATTN_DIM), 0) // t_new
    c_head = lax.broadcasted_iota(jnp.int32, (rows, ATTN_DIM), 1) // HEAD_DIM
    pad = jnp.zeros((PAGE_SIZE - t_new, ATTN_DIM), F32)

    q = new_ref[:, 0:ATTN_DIM] * ATTN_SCALE
    qhi, qlo = _split(jnp.where(r_head == c_head, jnp.concatenate([q] * N_HEADS, axis=0), 0.0))

    scores = []
    for n in range(n_blocks):
        ksum = None
        for pg in range(n * pages_per_block, (n + 1) * pages_per_block):
            kt = kt_refs[pg][...]
            lg_ref[pg] = _dot(qhi, kt.astype(BF16))
            ksum = kt if ksum is None else ksum + kt
        k_hi, k_lo = _split(ksum)
        y = _dot(qhi, k_hi) + _dot(qlo, k_hi) + _dot(qhi, k_lo)
        scores.append(jnp.sum(y, axis=1, keepdims=True))
    s = jnp.concatenate(scores, axis=1)
    lane_n = lax.broadcasted_iota(jnp.int32, (rows, n_blocks), 1)
    sel = _rank_select(s, lane_n, n_blocks, MOBA_TOPK)
    knew = jnp.concatenate([new_ref[:, ATTN_DIM:2 * ATTN_DIM], pad], axis=0)
    lg_ref[n_pages] = _dot_nt(qhi, knew.astype(BF16))

    mrun = jnp.full((rows, PAGE_SIZE), NEG, F32)
    for pg in range(n_pages + 1):
        lgt = lg_ref[pg] + sbias_ref[pg]
        if pg < n_pages:
            n = pg // pages_per_block
            lgt = jnp.where(sel[:, n:n + 1] > 0.5, lgt, NEG)
        lg_ref[pg] = lgt
        mrun = jnp.maximum(mrun, lgt)
    m = jnp.max(mrun, axis=1, keepdims=True)
    srun = jnp.zeros((rows, PAGE_SIZE), F32)
    for pg in range(n_pages + 1):
        e = jnp.exp(lg_ref[pg] - m)
        srun = srun + e
        p_ref[pg] = e.astype(BF16)
    linv = 1.0 / jnp.sum(srun, axis=1, keepdims=True)

    vnew = jnp.concatenate([new_ref[:, 2 * ATTN_DIM:3 * ATTN_DIM], pad], axis=0)
    acc = _dot(p_ref[n_pages], vnew.astype(BF16))
    for pg in range(n_pages):
        acc = acc + _dot_nt(p_ref[pg], vt_refs[pg][...].astype(BF16))
    acc = jnp.where(r_head == c_head, acc * linv, 0.0)
    out = acc[0:t_new, :]
    for h in range(1, N_HEADS):
        out = out + acc[h * t_new:(h + 1) * t_new, :]
    o_ref[...] = out


def _moba_sample(qkv, pool_kt, pool_vt, page_table, sbias, layer, n_pages):
    db, t, _ = qkv.shape
    rows = N_HEADS * t
    page_spec = lambda pg: pl.BlockSpec((None, None, ATTN_DIM, PAGE_SIZE), lambda b, pt: (layer, pt[b, pg], 0, 0))
    grid_spec = pltpu.PrefetchScalarGridSpec(
        num_scalar_prefetch=1,
        grid=(db,),
        in_specs=[pl.BlockSpec((None, t, QKV_DIM), lambda b, pt: (b, 0, 0))]
        + [page_spec(pg) for pg in range(n_pages)] * 2
        + [pl.BlockSpec((n_pages + 1, rows, PAGE_SIZE), lambda b, pt: (0, 0, 0))],
        out_specs=pl.BlockSpec((None, t, ATTN_DIM), lambda b, pt: (b, 0, 0)),
        scratch_shapes=[
            pltpu.VMEM((n_pages + 1, rows, PAGE_SIZE), F32),
            pltpu.VMEM((n_pages + 1, rows, PAGE_SIZE), BF16),
        ],
    )
    return pl.pallas_call(
        functools.partial(_moba_sample_kernel, n_pages=n_pages, t_new=t),
        grid_spec=grid_spec,
        out_shape=jax.ShapeDtypeStruct((db, t, ATTN_DIM), F32),
        compiler_params=_cparams(("parallel",)),
        name="moba_sample",
    )(page_table, qkv, *([pool_kt] * n_pages), *([pool_vt] * n_pages), sbias)


def _moba_sample_bias(bias_d, past_len, t):
    n_pages = past_len // PAGE_SIZE
    z = BIAS_PAD
    assert PAGE_SIZE <= BIAS_PAD and past_len + t <= BIAS_LEN
    past = jnp.stack([bias_d[:, z + tq + 1:z + tq + 1 + past_len][:, ::-1] for tq in range(t)], axis=1)
    past = past.reshape(N_HEADS * t, n_pages, PAGE_SIZE).transpose(1, 0, 2)
    new = jnp.stack([bias_d[:, z + tq - (PAGE_SIZE - 1):z + tq + 1][:, ::-1] for tq in range(t)], axis=1)
    return jnp.concatenate([past, new.reshape(1, N_HEADS * t, PAGE_SIZE)], axis=0)


def _dil_prompt_kernel(q_ref, kp_ref, kc_ref, vp_ref, vc_ref, bias_ref, o_ref, lse_ref):
    c = pl.program_id(2)
    ch = DIL_CHUNK
    col = lax.broadcasted_iota(jnp.int32, (ch, 2 * ch), 1)
    keep = (col >= ch) | (c > 0)
    for h in range(GROUP_HEADS):
        sl = slice(h * HEAD_DIM, (h + 1) * HEAD_DIM)
        q = (q_ref[:, sl] * ATTN_SCALE).astype(BF16)
        k = jnp.concatenate([kp_ref[:, sl], kc_ref[:, sl]], axis=0).astype(BF16)
        v = jnp.concatenate([vp_ref[:, sl], vc_ref[:, sl]], axis=0).astype(BF16)
        s = jnp.where(keep, _dot_nt(q, k) + bias_ref[h], NEG)
        m = jnp.max(s, axis=1, keepdims=True)
        p = jnp.exp(s - m)
        l = jnp.sum(p, axis=1, keepdims=True)
        o_ref[:, sl] = _dot(p.astype(BF16), v) / l
        lse_ref[:, sl] = jnp.broadcast_to(m + jnp.log(l), (ch, HEAD_DIM))


def _dil_prompt(qkv, bias, g, dil):
    b, s, _ = qkv.shape
    ch = DIL_CHUNK
    sub = s // dil
    nch = sub // ch
    nblk = QKV_DIM // GROUP_DIM
    view = qkv.reshape(b, sub, dil * QKV_DIM)
    kcol = ATTN_DIM // GROUP_DIM + g
    vcol = 2 * ATTN_DIM // GROUP_DIM + g
    spec = lambda colblk, prev: pl.BlockSpec(
        (None, ch, GROUP_DIM),
        (lambda bb, r, c: (bb, jnp.maximum(c - 1, 0), r * nblk + colblk)) if prev
        else (lambda bb, r, c: (bb, c, r * nblk + colblk)))
    out_spec = pl.BlockSpec((None, ch, GROUP_DIM), lambda bb, r, c: (bb, c, r))
    o, lse = pl.pallas_call(
        _dil_prompt_kernel,
        grid=(b, dil, nch),
        in_specs=[spec(g, False), spec(kcol, True), spec(kcol, False), spec(vcol, True), spec(vcol, False),
                  pl.BlockSpec((GROUP_HEADS, ch, 2 * ch), lambda bb, r, c: (0, 0, 0))],
        out_specs=[out_spec, out_spec],
        out_shape=[jax.ShapeDtypeStruct((b, sub, dil * GROUP_DIM), F32)] * 2,
        compiler_params=_cparams(("parallel", "parallel", "arbitrary")),
        name=f"dil_prompt{g}",
    )(view, view, view, view, view, bias)
    return o.reshape(b, s, GROUP_DIM), lse.reshape(b, s, GROUP_DIM)


def _dil_prompt_bias(bias_d, g, win, dil):
    ch = DIL_CHUNK
    assert win // dil == ch
    z = BIAS_PAD
    heads = bias_d[g * GROUP_HEADS:(g + 1) * GROUP_HEADS]
    steps = heads[:, z:z + win + 1:dil][:, ::-1]
    wp = jnp.concatenate([steps, jnp.full((GROUP_HEADS, 4 * ch - (ch + 1)), NEG, F32)], axis=1)
    return _toeplitz(wp, ch, 2 * ch)


def _dil_sample_kernel(new_ref, buf_ref, bb_ref, bn_ref, o_ref, lse_ref, *, g, dil, t_new):
    hp = 8
    r_head = lax.broadcasted_iota(jnp.int32, (hp, GROUP_DIM), 0)
    c_head = lax.broadcasted_iota(jnp.int32, (hp, GROUP_DIM), 1) // HEAD_DIM
    diag = r_head == c_head
    pad = jnp.zeros((DIL_CHUNK - t_new, GROUP_DIM), F32)
    qcol = g * GROUP_DIM
    kcol = ATTN_DIM + g * GROUP_DIM
    vcol = 2 * ATTN_DIM + g * GROUP_DIM
    knew = jnp.concatenate([new_ref[:, kcol:kcol + GROUP_DIM], pad], axis=0).astype(BF16)
    vnew = jnp.concatenate([new_ref[:, vcol:vcol + GROUP_DIM], pad], axis=0).astype(BF16)
    outs, lses = [], []
    for t in range(t_new):
        base = (t % dil) * 2 * GROUP_DIM
        kb = buf_ref[:, base:base + GROUP_DIM].astype(BF16)
        vb = buf_ref[:, base + GROUP_DIM:base + 2 * GROUP_DIM].astype(BF16)
        q = new_ref[t:t + 1, qcol:qcol + GROUP_DIM] * ATTN_SCALE
        qbd = jnp.where(diag, jnp.broadcast_to(q, (hp, GROUP_DIM)), 0.0).astype(BF16)
        lb = _dot_nt(qbd, kb) + bb_ref[t]
        ln = _dot_nt(qbd, knew) + bn_ref[t]
        m = jnp.maximum(jnp.max(lb, axis=1, keepdims=True), jnp.max(ln, axis=1, keepdims=True))
        pb = jnp.exp(lb - m)
        pn = jnp.exp(ln - m)
        l = jnp.sum(pb, axis=1, keepdims=True) + jnp.sum(pn, axis=1, keepdims=True)
        o8 = (_dot(pb.astype(BF16), vb) + _dot(pn.astype(BF16), vnew)) / l
        outs.append(jnp.sum(jnp.where(diag, o8, 0.0), axis=0, keepdims=True))
        lses.append(jnp.sum(jnp.where(diag, m + jnp.log(l), 0.0), axis=0, keepdims=True))
    o_ref[...] = jnp.concatenate(outs, axis=0)
    lse_ref[...] = jnp.concatenate(lses, axis=0)


def _dil_sample(qkv, state, bb, bn, layer, g, dil):
    db, t, _ = qkv.shape
    nb, _, lb = state.shape[:3]
    rows = lb // dil
    ncol = min(dil, t)
    view = state.reshape(nb, db, rows, dil * 2 * GROUP_DIM)
    out_spec = pl.BlockSpec((None, t, GROUP_DIM), lambda b: (b, 0, 0))
    return pl.pallas_call(
        functools.partial(_dil_sample_kernel, g=g, dil=dil, t_new=t),
        grid=(db,),
        in_specs=[
            pl.BlockSpec((None, t, QKV_DIM), lambda b: (b, 0, 0)),
            pl.BlockSpec((None, None, rows, ncol * 2 * GROUP_DIM), lambda b: (layer, b, 0, 0)),
            pl.BlockSpec((t, 8, rows), lambda b: (0, 0, 0)),
            pl.BlockSpec((t, 8, DIL_CHUNK), lambda b: (0, 0, 0)),
        ],
        out_specs=[out_spec, out_spec],
        out_shape=[jax.ShapeDtypeStruct((db, t, GROUP_DIM), F32)] * 2,
        compiler_params=_cparams(("parallel",)),
        name=f"dil_sample{g}",
    )(qkv, view, bb, bn)


def _dil_sample_bias(bias_d, g, win, dil, lb, t_new):
    rows = lb // dil
    z = BIAS_PAD
    assert rows == DIL_CHUNK and DIL_CHUNK <= BIAS_PAD
    heads = bias_d[g * GROUP_HEADS:(g + 1) * GROUP_HEADS]
    pad_rows = lambda a: jnp.concatenate([a, jnp.zeros((8 - GROUP_HEADS,) + a.shape[1:], F32)], axis=0)
    bb, bn = [], []
    m = np.arange(rows)
    tk = np.arange(DIL_CHUNK)
    for t in range(t_new):
        top = lb + t - t % dil
        vals = heads[:, z + top - (rows - 1) * dil:z + top + 1:dil][:, ::-1]
        bb.append(pad_rows(jnp.where(jnp.asarray(top - m * dil <= win), vals, NEG)))
        vals = heads[:, z + t - (DIL_CHUNK - 1):z + t + 1][:, ::-1]
        ok = (tk < t_new) & (t - tk >= 0) & ((t - tk) % dil == 0) & (t - tk <= win)
        bn.append(pad_rows(jnp.where(jnp.asarray(ok), vals, NEG)))
    return jnp.stack(bb), jnp.stack(bn)


def _oproj_kernel(h_ref, o_ref, gate_ref, whi_ref, wlo_ref, out_ref):
    out_ref[...] = h_ref[...] + gate_ref[...] * _dot3(o_ref[...], whi_ref[...], wlo_ref[...])


def _oproj_dil_kernel(h_ref, o0_ref, o1_ref, o2_ref, l0_ref, l1_ref, l2_ref, gate_ref, whi_ref, wlo_ref, out_ref):
    l0, l1, l2 = l0_ref[...], l1_ref[...], l2_ref[...]
    mx = jnp.maximum(jnp.maximum(l0, l1), l2)
    e0, e1, e2 = jnp.exp(l0 - mx), jnp.exp(l1 - mx), jnp.exp(l2 - mx)
    inv = 1.0 / (e0 + e1 + e2)
    o = jnp.concatenate([o0_ref[...] * (e0 * inv), o1_ref[...] * (e1 * inv), o2_ref[...] * (e2 * inv)], axis=1)
    out_ref[...] = h_ref[...] + gate_ref[...] * _dot3(o, whi_ref[...], wlo_ref[...])


def _oproj(h, attn, gate, w_hi, w_lo, tm, tiles_per_seg):
    n, d = h.shape
    row = lambda w: pl.BlockSpec((tm, w), lambda i: (i, 0))
    if isinstance(attn, tuple):
        kern, attn_args = _oproj_dil_kernel, list(attn[0]) + list(attn[1])
        attn_specs = [row(GROUP_DIM)] * 6
    else:
        kern, attn_args, attn_specs = _oproj_kernel, [attn], [row(ATTN_DIM)]
    return pl.pallas_call(
        kern,
        grid=(n // tm,),
        in_specs=[row(d)] + attn_specs + [
            _mod_spec(gate, tm, tiles_per_seg),
            pl.BlockSpec((ATTN_DIM, d), lambda i: (0, 0)),
            pl.BlockSpec((ATTN_DIM, d), lambda i: (0, 0)),
        ],
        out_specs=row(d),
        out_shape=jax.ShapeDtypeStruct((n, d), F32),
        compiler_params=_cparams(("parallel",)),
        name="oproj",
    )(h, *attn_args, gate, w_hi, w_lo)


def _topk_rows(src_ref, ridx, k, val_ref, idx_ref):
    nrows = ridx.shape[0]

    def body(it, carry):
        a = src_ref[0:nrows, :]
        mx = jnp.max(a, axis=0, keepdims=True)
        ix = jnp.min(jnp.where(a == mx, ridx, 1e9), axis=0, keepdims=True)
        src_ref[0:nrows, :] = jnp.where(ridx == ix, -jnp.inf, a)
        val_ref[pl.ds(it, 1), :] = mx
        idx_ref[pl.ds(it, 1), :] = ix
        return carry

    lax.fori_loop(0, k, body, 0)


def _cand_rows(k):
    return [(a, k // (a + 1)) for a in range(k)]


def _peer_select_kernel(x_ref, g_ref, sh_ref, sc_ref, whi_ref, wlo_ref, sk_ref, a_ref, b_ref, gate_ref,
                        qt_ref, s_ref, t1_ref, i1_ref, t2_ref, i2_ref, tc_ref, ic_ref):
    h = pl.program_id(1)
    k = PEER_TOPK
    half = PEER_DK // 2
    tm = x_ref.shape[0]

    @pl.when(h == 0)
    def _():
        f_hi, f_lo = _split(_norm_mod(x_ref[...], g_ref[...], sh_ref[...], sc_ref[...]))
        whi = whi_ref[...]
        qt_ref[...] = _dot_nt(whi, f_hi) + _dot_nt(wlo_ref[...], f_hi) + _dot_nt(whi, f_lo)

    key_rows = lax.broadcasted_iota(jnp.int32, (PEER_NKEYS, tm), 0).astype(F32)
    row0 = pl.multiple_of(h * PEER_DK, PEER_DK)
    for part, (t_ref, i_ref) in enumerate(((t1_ref, i1_ref), (t2_ref, i2_ref))):
        q_hi, q_lo = _split(qt_ref[pl.ds(row0 + part * half, half), :])
        s_ref[0:PEER_NKEYS, :] = _dot3(sk_ref[part], q_hi, q_lo)
        _topk_rows(s_ref, key_rows, k, t_ref, i_ref)
    off, flat = 0, []
    for a, nb in _cand_rows(k):
        s_ref[off:off + nb, :] = t1_ref[a:a + 1, :] + t2_ref[0:nb, :]
        flat.append(lax.broadcasted_iota(jnp.int32, (nb, tm), 0).astype(F32) + float(a * k))
        off += nb
    n_cand = -(-off // 8) * 8
    s_ref[off:n_cand, :] = jnp.full((n_cand - off, tm), -jnp.inf, F32)
    flat.append(jnp.full((n_cand - off, tm), float(k * k), F32))
    _topk_rows(s_ref, jnp.concatenate(flat, axis=0), k, tc_ref, ic_ref)
    jc = ic_ref[...]
    ja = jnp.floor(jc * (1.0 / k))
    jb = jc - ja * k
    e1 = jnp.zeros(jc.shape, F32)
    e2 = jnp.zeros(jc.shape, F32)
    for a in range(k):
        e1 = e1 + jnp.where(ja == float(a), i1_ref[a:a + 1, :], 0.0)
        e2 = e2 + jnp.where(jb == float(a), i2_ref[a:a + 1, :], 0.0)
    top = tc_ref[...]
    e = jnp.exp(top - top[0:1, :])
    a_ref[...] = e1
    b_ref[...] = e2
    gate_ref[...] = e / jnp.sum(e, axis=0, keepdims=True)


def _peer_select(x, g, shift, scale, wqt_hi, wqt_lo, sub_keys, tm, tiles_per_seg):
    n, d = x.shape
    k = PEER_TOPK
    nq = PEER_HEADS * PEER_DK
    out_spec = pl.BlockSpec((k, tm), lambda i, h: (h, i))
    out_shape = jax.ShapeDtypeStruct((PEER_HEADS * k, n), F32)
    mod_spec = lambda mod: pl.BlockSpec((None, mod.shape[1], d), lambda i, h: (i // tiles_per_seg, 0, 0))
    return pl.pallas_call(
        _peer_select_kernel,
        grid=(n // tm, PEER_HEADS),
        in_specs=[
            pl.BlockSpec((tm, d), lambda i, h: (i, 0)),
            pl.BlockSpec((1, d), lambda i, h: (0, 0)),
            mod_spec(shift),
            mod_spec(scale),
            pl.BlockSpec((nq, d), lambda i, h: (0, 0)),
            pl.BlockSpec((nq, d), lambda i, h: (0, 0)),
            pl.BlockSpec((2, PEER_NKEYS, PEER_DK // 2), lambda i, h: (0, 0, 0)),
        ],
        out_specs=[out_spec] * 3,
        out_shape=[out_shape] * 3,
        scratch_shapes=[pltpu.VMEM((nq, tm), F32), pltpu.VMEM((PEER_NKEYS, tm), F32)]
        + [pltpu.VMEM((k, tm), F32)] * 6,
        compiler_params=_cparams(("parallel", "arbitrary")),
        name="peer_select",
    )(x, g.reshape(1, d), shift, scale, wqt_hi, wqt_lo, sub_keys)


def _peer_dense_kernel(x_ref, g_ref, sh_ref, sc_ref, gt_ref, a_ref, b_ref, w_ref, u_ref, v_ref, out_ref,
                       f_ref, gs_ref, at_ref, bt_ref, wt_ref, p_ref, acc_ref):
    e = pl.program_id(1)
    tc = x_ref.shape[0]
    te = u_ref.shape[0]
    nk = PEER_NKEYS
    half = nk // 2
    npair = te // (2 * nk)
    hi_mask = jnp.int32(-65536)

    @pl.when(e == 0)
    def _():
        f_ref[...] = _norm_mod(x_ref[...], g_ref[...], sh_ref[...], sc_ref[...]).astype(BF16)
        acc_ref[...] = jnp.zeros(acc_ref.shape, F32)
        at_ref[...] = a_ref[...].T
        bt_ref[...] = b_ref[...].T
        wt_ref[...] = w_ref[...].T
        key = lax.broadcasted_iota(jnp.int32, (nk, nk), 0).astype(F32)

        def build(c, carry):
            first = jnp.where(key == at_ref[pl.ds(c, 1), :], 1.0, 0.0).astype(BF16)
            second = jnp.where(key == bt_ref[pl.ds(c, 1), :], wt_ref[pl.ds(c, 1), :], 0.0).astype(BF16)
            g = pltpu.bitcast(_dot_nt(first, second), jnp.int32) + 0x8000
            packed = (g[:half] & hi_mask) | lax.shift_right_logical(g[half:], jnp.full((half, nk), 16, jnp.int32))
            for k in range(half // 8):
                gs_ref[pl.ds(pl.multiple_of((k * tc + c) * 8, 8), 8), :] = packed[k * 8:(k + 1) * 8, :]
            return carry

        lax.fori_loop(0, tc, build, 0, unroll=16)

    for r in range(npair):
        p = e * npair + r
        packed = gs_ref[pl.ds((p // 8) * (tc * 8) + p % 8, tc, stride=8), :]
        g_a = pltpu.bitcast(packed & hi_mask, F32)
        g_b = pltpu.bitcast(packed << 16, F32)
        act = _dot_nt(f_ref[...], u_ref[r * 2 * nk:(r + 1) * 2 * nk, :])
        gelu = 0.5 * act * (1.0 + lax.erf(act * (2.0 ** -0.5)))
        p_ref[:, r * 2 * nk:r * 2 * nk + nk] = (g_a * gelu[:, :nk]).astype(BF16)
        p_ref[:, r * 2 * nk + nk:(r + 1) * 2 * nk] = (g_b * gelu[:, nk:]).astype(BF16)
    acc_ref[...] += _dot(p_ref[...], v_ref[...])

    @pl.when(e == pl.num_programs(1) - 1)
    def _():
        out_ref[...] = x_ref[...] + gt_ref[...] * acc_ref[...]


def _pair_permute(tab):
    half = PEER_NKEYS // 2
    t = tab.reshape(tab.shape[:-2] + (2, half, PEER_NKEYS, tab.shape[-1]))
    return jnp.swapaxes(t, -4, -3).reshape(tab.shape).astype(BF16)


def _peer_dense(x, g, shift, scale, gate, first, second, weight, u_tab, v_tab, tc, te, tiles_per_seg):
    n, d = x.shape
    n_exp = u_tab.shape[0]
    nhk = PEER_HEADS * PEER_TOPK
    mod_spec = lambda mod: pl.BlockSpec((None, mod.shape[1], d), lambda i, e: (i // tiles_per_seg, 0, 0))
    sel_spec = pl.BlockSpec((nhk, tc), lambda i, e: (0, i))
    return pl.pallas_call(
        _peer_dense_kernel,
        grid=(n // tc, n_exp // te),
        in_specs=[
            pl.BlockSpec((tc, d), lambda i, e: (i, 0)),
            pl.BlockSpec((1, d), lambda i, e: (0, 0)),
            mod_spec(shift), mod_spec(scale), mod_spec(gate),
            sel_spec, sel_spec, sel_spec,
            pl.BlockSpec((te, d), lambda i, e: (e, 0)),
            pl.BlockSpec((te, d), lambda i, e: (e, 0)),
        ],
        out_specs=pl.BlockSpec((tc, d), lambda i, e: (i, 0)),
        out_shape=jax.ShapeDtypeStruct((n, d), F32),
        scratch_shapes=[
            pltpu.VMEM((tc, d), BF16),
            pltpu.VMEM((tc * PEER_NKEYS // 2, PEER_NKEYS), jnp.int32),
            pltpu.VMEM((tc, nhk), F32),
            pltpu.VMEM((tc, nhk), F32),
            pltpu.VMEM((tc, nhk), F32),
            pltpu.VMEM((tc, te), BF16),
            pltpu.VMEM((tc, d), F32),
        ],
        compiler_params=_cparams(("parallel", "arbitrary")),
        name="peer_dense",
    )(x, g.reshape(1, d), shift, scale, gate, first, second, weight, u_tab, v_tab)


def _final_norm_kernel(x_ref, g_ref, o_ref):
    x = x_ref[...]
    ms = jnp.mean(x * x, axis=-1, keepdims=True)
    o_ref[...] = x * lax.rsqrt(ms + RMS_EPS) * g_ref[...]


def _final_norm(x, g, tm):
    n, d = x.shape
    return pl.pallas_call(
        _final_norm_kernel,
        grid=(n // tm,),
        in_specs=[pl.BlockSpec((tm, d), lambda i: (i, 0)), pl.BlockSpec((1, d), lambda i: (0, 0))],
        out_specs=pl.BlockSpec((tm, d), lambda i: (i, 0)),
        out_shape=jax.ShapeDtypeStruct((n, d), F32),
        compiler_params=_cparams(("parallel",)),
        name="final_norm",
    )(x, g.reshape(1, d))


TM = 256
PEER_TC = 512
PEER_TE = 2048


def _split_w(w):
    hi = w.astype(BF16)
    return hi, (w - hi.astype(F32)).astype(BF16)


def kernel(x_prompt, x_sample, cache_k_moba, cache_v_moba, state_kv_dil0, state_kv_dil1, state_kv_dil2, page_table,
           c_prompt, c_sample, w_ada, b_ada, g_norm, w_qkv, w_o, rel_bias, w_pq, peer_sub_keys, peer_u, peer_v,
           g_final):
    B, S, D = x_prompt.shape
    DB, T, _ = x_sample.shape
    n_a, n_pool = cache_k_moba.shape[:2]
    n_pages = page_table.shape[1]
    past_len = n_pages * PAGE_SIZE
    assert D == D_MODEL and S % MOBA_BLOCK == 0 and past_len % MOBA_BLOCK == 0 and S % TM == 0
    dil_states = (state_kv_dil0, state_kv_dil1, state_kv_dil2)
    for st, (win, dil) in zip(dil_states, DIL_GROUPS):
        assert st.shape[2] == win and win // dil == DIL_CHUNK and S % (dil * DIL_CHUNK) == 0

    np_, ns_ = B * S, DB * T
    tms = min(TM, ns_)
    hp = x_prompt.reshape(np_, D)
    hs = x_sample.reshape(ns_, D)

    mod_all = _ada(jnp.concatenate([c_prompt, c_sample], axis=0), w_ada, b_ada)
    wqkv_hi, wqkv_lo = _split_w(w_qkv)
    wo_hi, wo_lo = _split_w(w_o)
    wpqt_hi, wpqt_lo = _split_w(jnp.swapaxes(w_pq, 1, 2))
    u_bf, v_bf = _pair_permute(peer_u), _pair_permute(peer_v)
    pool_kt = cache_k_moba.transpose(0, 1, 3, 4, 2).reshape(n_a, n_pool, ATTN_DIM, PAGE_SIZE)
    pool_vt = cache_v_moba.transpose(0, 1, 3, 4, 2).reshape(n_a, n_pool, ATTN_DIM, PAGE_SIZE)

    bias_d = _bias_by_distance(rel_bias)
    moba_bias_p = _moba_prompt_bias(bias_d, S)
    moba_bias_s = _moba_sample_bias(bias_d, past_len, T)
    dil_bias_p = [_dil_prompt_bias(bias_d, g, w, d) for g, (w, d) in enumerate(DIL_GROUPS)]
    dil_bias_s = [_dil_sample_bias(bias_d, g, w, d, dil_states[g].shape[2], T)
                  for g, (w, d) in enumerate(DIL_GROUPS)]

    kp_rows, vp_rows, ks_rows, vs_rows = [], [], [], []
    dil_p = [[] for _ in DIL_GROUPS]
    dil_s = [[] for _ in DIL_GROUPS]
    for i in range(DEPTH):
        j = i // 2
        mod = mod_all[i].reshape(B + DB, 6, D)
        mp = [mod[:B, c].reshape(B, 1, D) for c in range(6)]
        ms = [jnp.repeat(mod[B:, c], T, axis=0).reshape(ns_ // tms, tms, D) for c in range(6)]
        seg_p, seg_s = S // TM, 1

        qkv_p = _nmm(hp, g_norm[i, 0], mp[0], mp[1], wqkv_hi[i], wqkv_lo[i], TM, seg_p)
        qkv_s = _nmm(hs, g_norm[i, 0], ms[0], ms[1], wqkv_hi[i], wqkv_lo[i], tms, seg_s)
        qkv_p3 = qkv_p.reshape(B, S, QKV_DIM)
        qkv_s3 = qkv_s.reshape(DB, T, QKV_DIM)
        heads = lambda a, n, l: a.reshape(n, l, 3, N_HEADS, HEAD_DIM)
        qkv_p5, qkv_s5 = heads(qkv_p, B, S), heads(qkv_s, DB, T)
        if i % 2 == 0:
            attn_p = _moba_prompt(qkv_p3, moba_bias_p).reshape(np_, ATTN_DIM)
            attn_s = _moba_sample(qkv_s3, pool_kt, pool_vt, page_table, moba_bias_s, j, n_pages).reshape(ns_, ATTN_DIM)
            kp_rows.append(qkv_p5[:, :, 1])
            vp_rows.append(qkv_p5[:, :, 2])
            ks_rows.append(qkv_s5[:, :, 1])
            vs_rows.append(qkv_s5[:, :, 2])
        else:
            op, lp, os_, ls = [], [], [], []
            for g, (win, dil) in enumerate(DIL_GROUPS):
                o, lse = _dil_prompt(qkv_p3, dil_bias_p[g], g, dil)
                op.append(o.reshape(np_, GROUP_DIM))
                lp.append(lse.reshape(np_, GROUP_DIM))
                o, lse = _dil_sample(qkv_s3, dil_states[g], *dil_bias_s[g], j, g, dil)
                os_.append(o.reshape(ns_, GROUP_DIM))
                ls.append(lse.reshape(ns_, GROUP_DIM))
                hsl = slice(g * GROUP_HEADS, (g + 1) * GROUP_HEADS)
                kv_p = jnp.stack([qkv_p5[:, :, 1, hsl], qkv_p5[:, :, 2, hsl]], axis=2)
                dil_p[g].append(kv_p[:, S - min(win, S):])
                kv_s = jnp.stack([qkv_s5[:, :, 1, hsl], qkv_s5[:, :, 2, hsl]], axis=2)
                kv_s = jnp.concatenate([dil_states[g][j], kv_s], axis=1)
                dil_s[g].append(kv_s[:, kv_s.shape[1] - min(win, past_len + T):])
            attn_p, attn_s = (op, lp), (os_, ls)
        hp = _oproj(hp, attn_p, mp[2], wo_hi[i], wo_lo[i], TM, seg_p)
        hs = _oproj(hs, attn_s, ms[2], wo_hi[i], wo_lo[i], tms, seg_s)

        sel_p = _peer_select(hp, g_norm[i, 1], mp[3], mp[4], wpqt_hi[i], wpqt_lo[i], peer_sub_keys[i], TM, seg_p)
        sel_s = _peer_select(hs, g_norm[i, 1], ms[3], ms[4], wpqt_hi[i], wpqt_lo[i], peer_sub_keys[i], tms, seg_s)
        hp = _peer_dense(hp, g_norm[i, 1], mp[3], mp[4], mp[5], *sel_p, u_bf[i], v_bf[i], PEER_TC, PEER_TE,
                         S // PEER_TC)
        hs = _peer_dense(hs, g_norm[i, 1], ms[3], ms[4], ms[5], *sel_s, u_bf[i], v_bf[i], tms, PEER_TE, 1)

    y_prompt = _final_norm(hp, g_final, TM).reshape(B, S, D)
    y_sample = _final_norm(hs, g_final, tms).reshape(DB, T, D)
    return (y_prompt, y_sample,
            jnp.stack(kp_rows), jnp.stack(vp_rows), jnp.stack(ks_rows), jnp.stack(vs_rows),
            jnp.stack(dil_p[0]), jnp.stack(dil_p[1]), jnp.stack(dil_p[2]),
            jnp.stack(dil_s[0]), jnp.stack(dil_s[1]), jnp.stack(dil_s[2]))
```

```python
import functools
import math

import numpy as np
import jax
import jax.numpy as jnp
from jax import lax
from jax.experimental import pallas as pl
from jax.experimental.pallas import tpu as pltpu

F32 = jnp.float32
BF16 = jnp.bfloat16
NEG = -1e30

D_MODEL = 1024
N_HEADS = 12
HEAD_DIM = 64
ATTN_DIM = N_HEADS * HEAD_DIM
QKV_DIM = 3 * ATTN_DIM
ATTN_SCALE = HEAD_DIM ** -0.5
DEPTH = 4
PAGE_SIZE = 128
MOBA_BLOCK = 256
MOBA_TOPK = 3
DIL_GROUPS = ((128, 1), (512, 4), (2048, 16))
GROUP_HEADS = 4
GROUP_DIM = GROUP_HEADS * HEAD_DIM
DIL_CHUNK = 128
REL_BUCKETS = 32
REL_MAX_DIST = 2048
PEER_HEADS = 8
PEER_NKEYS = 128
PEER_DK = 256
PEER_TOPK = 16
RMS_EPS = 1e-6

VMEM_LIMIT = 56 * 1024 * 1024


def _cparams(sem):
    return pltpu.CompilerParams(dimension_semantics=sem, vmem_limit_bytes=VMEM_LIMIT)


def _split(x):
    hi = x.astype(BF16)
    lo = (x - hi.astype(F32)).astype(BF16)
    return hi, lo


def _dot(a, b):
    return jnp.dot(a, b, preferred_element_type=F32)


def _dot_nt(a, b):
    return lax.dot_general(a, b, (((1,), (1,)), ((), ())), preferred_element_type=F32)


def _dot3(a, b_hi, b_lo):
    a_hi, a_lo = _split(a)
    return _dot(a_hi, b_hi) + _dot(a_lo, b_hi) + _dot(a_hi, b_lo)


def _dot3_nt(a, b):
    a_hi, a_lo = _split(a)
    b_hi, b_lo = _split(b)
    return _dot_nt(a_hi, b_hi) + _dot_nt(a_lo, b_hi) + _dot_nt(a_hi, b_lo)


def _norm_mod(x, g, shift, scale):
    ms = jnp.mean(x * x, axis=-1, keepdims=True)
    xn = x * lax.rsqrt(ms + RMS_EPS) * g
    return xn * (1.0 + scale) + shift


def _bucket_np(dist):
    n = np.maximum(dist, 0)
    max_exact = REL_BUCKETS // 2
    nf = np.maximum(n, 1).astype(np.float64)
    large = max_exact + (np.log(nf / max_exact) / math.log(REL_MAX_DIST / max_exact)
                         * (REL_BUCKETS - max_exact)).astype(np.int64)
    large = np.minimum(large, REL_BUCKETS - 1)
    return np.where(n < max_exact, n, large).astype(np.int32)


BIAS_PAD = 256
BIAS_LEN = 4352


def _bias_by_distance(rel_bias):
    onehot = np.zeros((BIAS_LEN, REL_BUCKETS), np.float32)
    onehot[np.arange(BIAS_LEN), _bucket_np(np.arange(BIAS_LEN))] = 1.0
    by_dist = jnp.dot(jnp.asarray(onehot), rel_bias, precision=lax.Precision.HIGHEST)
    return jnp.concatenate([jnp.full((N_HEADS, BIAS_PAD), NEG, F32), by_dist.T], axis=1)


def _toeplitz(wp, rows, cols):
    n2 = wp.shape[-1]
    x = jnp.tile(wp, (1,) * (wp.ndim - 1) + (rows,))[..., :rows * (n2 - 1)]
    return x.reshape(wp.shape[:-1] + (rows, n2 - 1))[..., :cols]


def _rank_select(s, lane_n, n_valid, topk):
    nb = s.shape[1]
    rank = jnp.zeros(s.shape, F32)
    for m in range(nb):
        sm = s[:, m:m + 1]
        beats = (sm > s) | ((sm == s) & (m < lane_n))
        beats = beats & (m < n_valid)
        rank = rank + jnp.where(beats, 1.0, 0.0)
    sel = (lane_n < n_valid) & (rank < float(topk))
    return jnp.where(sel, 1.0, 0.0)


def _rank_select_rows(s, row_n, n_valid, topk):
    nb = s.shape[0]
    rank = jnp.zeros(s.shape, F32)
    for m in range(nb):
        sm = s[m:m + 1, :]
        beats = (sm > s) | ((sm == s) & (m < row_n))
        beats = beats & (m < n_valid)
        rank = rank + jnp.where(beats, 1.0, 0.0)
    sel = (row_n < n_valid) & (rank < float(topk))
    return jnp.where(sel, 1.0, 0.0)


def _ada_kernel(c_ref, w_ref, b_ref, o_ref):
    c = c_ref[...]
    s = c * (1.0 / (1.0 + jnp.exp(-c)))
    w_hi, w_lo = _split(w_ref[...])
    o_ref[...] = _dot3(s, w_hi, w_lo) + b_ref[...]


def _ada(c_all, w_ada, b_ada):
    n = c_all.shape[0]
    depth, d, n6 = w_ada.shape
    tn = 1536
    return pl.pallas_call(
        _ada_kernel,
        grid=(depth, n6 // tn),
        in_specs=[
            pl.BlockSpec((n, d), lambda i, j: (0, 0)),
            pl.BlockSpec((None, d, tn), lambda i, j: (i, 0, j)),
            pl.BlockSpec((None, 1, tn), lambda i, j: (i, 0, j)),
        ],
        out_specs=pl.BlockSpec((None, n, tn), lambda i, j: (i, 0, j)),
        out_shape=jax.ShapeDtypeStruct((depth, n, n6), F32),
        compiler_params=_cparams(("parallel", "parallel")),
        name="ada",
    )(c_all, w_ada, b_ada.reshape(depth, 1, n6))


def _nmm_kernel(x_ref, g_ref, sh_ref, sc_ref, whi_ref, wlo_ref, o_ref):
    f = _norm_mod(x_ref[...], g_ref[...], sh_ref[...], sc_ref[...])
    o_ref[...] = _dot3(f, whi_ref[...], wlo_ref[...])


def _mod_spec(mod, tm, tiles_per_seg):
    r = mod.shape[1]
    return pl.BlockSpec((None, r, D_MODEL), lambda i, *_: (i // tiles_per_seg, 0, 0))


def _nmm(x, g, shift, scale, w_hi, w_lo, tm, tiles_per_seg):
    n, d = x.shape
    nout = w_hi.shape[1]
    return pl.pallas_call(
        _nmm_kernel,
        grid=(n // tm,),
        in_specs=[
            pl.BlockSpec((tm, d), lambda i: (i, 0)),
            pl.BlockSpec((1, d), lambda i: (0, 0)),
            _mod_spec(shift, tm, tiles_per_seg),
            _mod_spec(scale, tm, tiles_per_seg),
            pl.BlockSpec((d, nout), lambda i: (0, 0)),
            pl.BlockSpec((d, nout), lambda i: (0, 0)),
        ],
        out_specs=pl.BlockSpec((tm, nout), lambda i: (i, 0)),
        out_shape=jax.ShapeDtypeStruct((n, nout), F32),
        compiler_params=_cparams(("parallel",)),
        name="nmm",
    )(x, g.reshape(1, d), shift, scale, w_hi, w_lo)


def _moba_prompt_kernel(q_ref, k_ref, v_ref, bias_ref, o_ref, kmean_ref, mask_ref):
    i = pl.program_id(2)
    blk = MOBA_BLOCK
    nb = k_ref.shape[0] // blk

    @pl.when(i == 0)
    def _():
        for n in range(nb):
            kmean_ref[n:n + 1, :] = jnp.mean(k_ref[n * blk:(n + 1) * blk, :], axis=0, keepdims=True)

    lane_head = lax.broadcasted_iota(jnp.int32, (blk, 128), 1) // HEAD_DIM
    row_n = lax.broadcasted_iota(jnp.int32, (nb, blk), 0)
    km_head = lax.broadcasted_iota(jnp.int32, (nb, 128), 1) // HEAD_DIM
    q2 = q_ref[...] * ATTN_SCALE
    qb = []
    for hh in range(2):
        q = jnp.where(lane_head == hh, q2, 0.0)
        scores = _dot3_nt(jnp.where(km_head == hh, kmean_ref[...], 0.0), q)
        sel = _rank_select_rows(scores, row_n, i, MOBA_TOPK)
        sel = jnp.concatenate([sel, jnp.zeros((128 - nb, blk), F32)], axis=0).T
        for n in range(nb):
            mask_ref[hh, n] = jnp.broadcast_to(sel[:, n:n + 1], (blk, 128))
        qb.append(q.astype(BF16))

    def tile(j, hh, masked):
        r0 = pl.multiple_of(j * blk, blk)
        logits = _dot_nt(qb[hh], k_ref[pl.ds(r0, blk), :].astype(BF16)) + bias_ref[i - j, hh]
        if masked:
            keep = mask_ref[hh, j] > 0.5
            logits = jnp.where(jnp.concatenate([keep] * (blk // 128), axis=1), logits, NEG)
        return logits, v_ref[pl.ds(r0, blk), :].astype(BF16)

    state = []
    for hh in range(2):
        logits, vb = tile(i, hh, False)
        m = jnp.max(logits, axis=1, keepdims=True)
        p = jnp.exp(logits - m)
        state += [m, jnp.sum(p, axis=1, keepdims=True), _dot(p.astype(BF16), vb)]

    def body(j, carry):
        out = []
        for hh in range(2):
            m, l, acc = carry[3 * hh:3 * hh + 3]
            logits, vb = tile(j, hh, True)
            m_new = jnp.maximum(m, jnp.max(logits, axis=1, keepdims=True))
            alpha = jnp.exp(m - m_new)
            p = jnp.exp(logits - m_new)
            out += [m_new, alpha * l + jnp.sum(p, axis=1, keepdims=True), alpha * acc + _dot(p.astype(BF16), vb)]
        return tuple(out)

    _, l0, acc0, _, l1, acc1 = lax.fori_loop(0, i, body, tuple(state))
    o_ref[...] = jnp.where(lane_head == 0, acc0 / l0, acc1 / l1)


def _moba_prompt(qkv, bias):
    b, s, _ = qkv.shape
    blk = MOBA_BLOCK
    nb = s // blk
    npair = N_HEADS // 2
    return pl.pallas_call(
        _moba_prompt_kernel,
        grid=(npair, b, nb),
        in_specs=[
            pl.BlockSpec((None, blk, 128), lambda hp, bb, i: (bb, i, hp)),
            pl.BlockSpec((None, s, 128), lambda hp, bb, i: (bb, 0, npair + hp)),
            pl.BlockSpec((None, s, 128), lambda hp, bb, i: (bb, 0, 2 * npair + hp)),
            pl.BlockSpec((nb, 2, blk, blk), lambda hp, bb, i: (0, hp, 0, 0)),
        ],
        out_specs=pl.BlockSpec((None, blk, 128), lambda hp, bb, i: (bb, i, hp)),
        out_shape=jax.ShapeDtypeStruct((b, s, ATTN_DIM), F32),
        scratch_shapes=[pltpu.VMEM((nb, 128), F32), pltpu.VMEM((2, nb, blk, 128), F32)],
        compiler_params=_cparams(("parallel", "parallel", "arbitrary")),
        name="moba_prompt",
    )(qkv, qkv, qkv, bias)


def _moba_prompt_bias(bias_d, s):
    blk = MOBA_BLOCK
    nb = s // blk
    assert blk <= BIAS_PAD
    wps = []
    for delta in range(nb):
        z = BIAS_PAD + delta * blk
        lo = bias_d[:, z - blk:z + 1][:, ::-1]
        hi = bias_d[:, z + 1:z + blk][:, ::-1]
        wps.append(jnp.concatenate([lo, hi], axis=1))
    return _toeplitz(jnp.stack(wps), blk, blk)


def _moba_sample_kernel(pt_ref, new_ref, *refs, n_pages, t_new):
    del pt_ref
    kt_refs, vt_refs = refs[:n_pages], refs[n_pages:2 * n_pages]
    sbias_ref, o_ref, lg_ref, p_ref = refs[2 * n_pages:]
    rows = N_HEADS * t_new
    pages_per_block = MOBA_BLOCK // PAGE_SIZE
    n_blocks = n_pages // pages_per_block
    r_head = lax.broadcasted_iota(jnp.int32, (rows, ATTN_DIM), 0) // t_new
    c_head = lax.broadcasted_iota(jnp.int32, (rows, ATTN_DIM), 1) // HEAD_DIM
    pad = jnp.zeros((PAGE_SIZE - t_new, ATTN_DIM), F32)

    q = new_ref[:, 0:ATTN_DIM] * ATTN_SCALE
    qhi, qlo = _split(jnp.where(r_head == c_head, jnp.concatenate([q] * N_HEADS, axis=0), 0.0))

    scores = []
    for n in range(n_blocks):
        ksum = None
        for pg in range(n * pages_per_block, (n + 1) * pages_per_block):
            kt = kt_refs[pg][...]
            lg_ref[pg] = _dot(qhi, kt.astype(BF16))
            ksum = kt if ksum is None else ksum + kt
        k_hi, k_lo = _split(ksum)
        y = _dot(qhi, k_hi) + _dot(qlo, k_hi) + _dot(qhi, k_lo)
        scores.append(jnp.sum(y, axis=1, keepdims=True))
    s = jnp.concatenate(scores, axis=1)
    lane_n = lax.broadcasted_iota(jnp.int32, (rows, n_blocks), 1)
    sel = _rank_select(s, lane_n, n_blocks, MOBA_TOPK)
    knew = jnp.concatenate([new_ref[:, ATTN_DIM:2 * ATTN_DIM], pad], axis=0)
    lg_ref[n_pages] = _dot_nt(qhi, knew.astype(BF16))

    mrun = jnp.full((rows, PAGE_SIZE), NEG, F32)
    for pg in range(n_pages + 1):
        lgt = lg_ref[pg] + sbias_ref[pg]
        if pg < n_pages:
            n = pg // pages_per_block
            lgt = jnp.where(sel[:, n:n + 1] > 0.5, lgt, NEG)
        lg_ref[pg] = lgt
        mrun = jnp.maximum(mrun, lgt)
    m = jnp.max(mrun, axis=1, keepdims=True)
    srun = jnp.zeros((rows, PAGE_SIZE), F32)
    for pg in range(n_pages + 1):
        e = jnp.exp(lg_ref[pg] - m)
        srun = srun + e
        p_ref[pg] = e.astype(BF16)
    linv = 1.0 / jnp.sum(srun, axis=1, keepdims=True)

    vnew = jnp.concatenate([new_ref[:, 2 * ATTN_DIM:3 * ATTN_DIM], pad], axis=0)
    acc = _dot(p_ref[n_pages], vnew.astype(BF16))
    for pg in range(n_pages):
        acc = acc + _dot_nt(p_ref[pg], vt_refs[pg][...].astype(BF16))
    acc = jnp.where(r_head == c_head, acc * linv, 0.0)
    out = acc[0:t_new, :]
    for h in range(1, N_HEADS):
        out = out + acc[h * t_new:(h + 1) * t_new, :]
    o_ref[...] = out


def _moba_sample(qkv, pool_kt, pool_vt, page_table, sbias, layer, n_pages):
    db, t, _ = qkv.shape
    rows = N_HEADS * t
    page_spec = lambda pg: pl.BlockSpec((None, None, ATTN_DIM, PAGE_SIZE), lambda b, pt: (layer, pt[b, pg], 0, 0))
    grid_spec = pltpu.PrefetchScalarGridSpec(
        num_scalar_prefetch=1,
        grid=(db,),
        in_specs=[pl.BlockSpec((None, t, QKV_DIM), lambda b, pt: (b, 0, 0))]
        + [page_spec(pg) for pg in range(n_pages)] * 2
        + [pl.BlockSpec((n_pages + 1, rows, PAGE_SIZE), lambda b, pt: (0, 0, 0))],
        out_specs=pl.BlockSpec((None, t, ATTN_DIM), lambda b, pt: (b, 0, 0)),
        scratch_shapes=[
            pltpu.VMEM((n_pages + 1, rows, PAGE_SIZE), F32),
            pltpu.VMEM((n_pages + 1, rows, PAGE_SIZE), BF16),
        ],
    )
    return pl.pallas_call(
        functools.partial(_moba_sample_kernel, n_pages=n_pages, t_new=t),
        grid_spec=grid_spec,
        out_shape=jax.ShapeDtypeStruct((db, t, ATTN_DIM), F32),
        compiler_params=_cparams(("parallel",)),
        name="moba_sample",
    )(page_table, qkv, *([pool_kt] * n_pages), *([pool_vt] * n_pages), sbias)


def _moba_sample_bias(bias_d, past_len, t):
    n_pages = past_len // PAGE_SIZE
    z = BIAS_PAD
    assert PAGE_SIZE <= BIAS_PAD and past_len + t <= BIAS_LEN
    past = jnp.stack([bias_d[:, z + tq + 1:z + tq + 1 + past_len][:, ::-1] for tq in range(t)], axis=1)
    past = past.reshape(N_HEADS * t, n_pages, PAGE_SIZE).transpose(1, 0, 2)
    new = jnp.stack([bias_d[:, z + tq - (PAGE_SIZE - 1):z + tq + 1][:, ::-1] for tq in range(t)], axis=1)
    return jnp.concatenate([past, new.reshape(1, N_HEADS * t, PAGE_SIZE)], axis=0)


def _dil_prompt_kernel(q_ref, kp_ref, kc_ref, vp_ref, vc_ref, bias_ref, o_ref, lse_ref):
    c = pl.program_id(2)
    ch = DIL_CHUNK
    col = lax.broadcasted_iota(jnp.int32, (ch, 2 * ch), 1)
    keep = (col >= ch) | (c > 0)
    for h in range(GROUP_HEADS):
        sl = slice(h * HEAD_DIM, (h + 1) * HEAD_DIM)
        q = (q_ref[:, sl] * ATTN_SCALE).astype(BF16)
        k = jnp.concatenate([kp_ref[:, sl], kc_ref[:, sl]], axis=0).astype(BF16)
        v = jnp.concatenate([vp_ref[:, sl], vc_ref[:, sl]], axis=0).astype(BF16)
        s = jnp.where(keep, _dot_nt(q, k) + bias_ref[h], NEG)
        m = jnp.max(s, axis=1, keepdims=True)
        p = jnp.exp(s - m)
        l = jnp.sum(p, axis=1, keepdims=True)
        o_ref[:, sl] = _dot(p.astype(BF16), v) / l
        lse_ref[:, sl] = jnp.broadcast_to(m + jnp.log(l), (ch, HEAD_DIM))


def _dil_prompt(qkv, bias, g, dil):
    b, s, _ = qkv.shape
    ch = DIL_CHUNK
    sub = s // dil
    nch = sub // ch
    nblk = QKV_DIM // GROUP_DIM
    view = qkv.reshape(b, sub, dil * QKV_DIM)
    kcol = ATTN_DIM // GROUP_DIM + g
    vcol = 2 * ATTN_DIM // GROUP_DIM + g
    spec = lambda colblk, prev: pl.BlockSpec(
        (None, ch, GROUP_DIM),
        (lambda bb, r, c: (bb, jnp.maximum(c - 1, 0), r * nblk + colblk)) if prev
        else (lambda bb, r, c: (bb, c, r * nblk + colblk)))
    out_spec = pl.BlockSpec((None, ch, GROUP_DIM), lambda bb, r, c: (bb, c, r))
    o, lse = pl.pallas_call(
        _dil_prompt_kernel,
        grid=(b, dil, nch),
        in_specs=[spec(g, False), spec(kcol, True), spec(kcol, False), spec(vcol, True), spec(vcol, False),
                  pl.BlockSpec((GROUP_HEADS, ch, 2 * ch), lambda bb, r, c: (0, 0, 0))],
        out_specs=[out_spec, out_spec],
        out_shape=[jax.ShapeDtypeStruct((b, sub, dil * GROUP_DIM), F32)] * 2,
        compiler_params=_cparams(("parallel", "parallel", "arbitrary")),
        name=f"dil_prompt{g}",
    )(view, view, view, view, view, bias)
    return o.reshape(b, s, GROUP_DIM), lse.reshape(b, s, GROUP_DIM)


def _dil_prompt_bias(bias_d, g, win, dil):
    ch = DIL_CHUNK
    assert win // dil == ch
    z = BIAS_PAD
    heads = bias_d[g * GROUP_HEADS:(g + 1) * GROUP_HEADS]
    steps = heads[:, z:z + win + 1:dil][:, ::-1]
    wp = jnp.concatenate([steps, jnp.full((GROUP_HEADS, 4 * ch - (ch + 1)), NEG, F32)], axis=1)
    return _toeplitz(wp, ch, 2 * ch)


def _dil_sample_kernel(new_ref, buf_ref, bb_ref, bn_ref, *rest, g, t_new):
    o_ref, lse_ref, st_ref = rest[-3:]
    lb = buf_ref.shape[1]
    rows = GROUP_HEADS * t_new
    ch = DIL_CHUNK
    r_head = lax.broadcasted_iota(jnp.int32, (rows, GROUP_DIM), 0) // t_new
    c_head = lax.broadcasted_iota(jnp.int32, (rows, GROUP_DIM), 1) // HEAD_DIM
    diag = r_head == c_head
    pad = jnp.zeros((ch - t_new, GROUP_DIM), F32)
    qcol = g * GROUP_DIM
    kcol = ATTN_DIM + g * GROUP_DIM
    vcol = 2 * ATTN_DIM + g * GROUP_DIM
    knew, vnew = new_ref[:, kcol:kcol + GROUP_DIM], new_ref[:, vcol:vcol + GROUP_DIM]
    q = new_ref[:, qcol:qcol + GROUP_DIM] * ATTN_SCALE
    qbd = jnp.where(diag, jnp.concatenate([q] * GROUP_HEADS, axis=0), 0.0).astype(BF16)
    buf = buf_ref[...]
    kt, vt = buf[:GROUP_DIM].astype(BF16), buf[GROUP_DIM:].astype(BF16)
    lgb = _dot(qbd, kt) + bb_ref[...]
    lgn = _dot_nt(qbd, jnp.concatenate([knew, pad], axis=0).astype(BF16)) + bn_ref[...]
    m = jnp.maximum(jnp.max(lgb, axis=1, keepdims=True), jnp.max(lgn, axis=1, keepdims=True))
    pb = jnp.exp(lgb - m)
    pn = jnp.exp(lgn - m)
    l = jnp.sum(pb, axis=1, keepdims=True) + jnp.sum(pn, axis=1, keepdims=True)
    o = _dot_nt(pb.astype(BF16), vt) + _dot(pn.astype(BF16), jnp.concatenate([vnew, pad], axis=0).astype(BF16))
    o = jnp.where(diag, o / l, 0.0)
    lse = jnp.where(diag, m + jnp.log(l), 0.0)
    o_ref[...] = sum(o[h * t_new:(h + 1) * t_new] for h in range(GROUP_HEADS))
    lse_ref[...] = sum(lse[h * t_new:(h + 1) * t_new] for h in range(GROUP_HEADS))

    tail = jnp.concatenate([jnp.zeros((ch - t_new, 2 * GROUP_DIM), F32),
                            jnp.concatenate([knew, vnew], axis=1)], axis=0).T
    shifted = pltpu.roll(buf, lb - t_new, axis=1)
    lane = lax.broadcasted_iota(jnp.int32, (2 * GROUP_DIM, ch), 1)
    if lb > ch:
        st_ref[:, 0:lb - ch] = shifted[:, 0:lb - ch]
    st_ref[:, lb - ch:lb] = jnp.where(lane >= ch - t_new, tail, shifted[:, lb - ch:lb])


def _dil_sample(qkv, state_t, prev_out, bb, bn, layer, g):
    db, t, _ = qkv.shape
    nb, _, kv_dim, lb = state_t.shape
    rows = GROUP_HEADS * t
    assert t <= DIL_CHUNK <= lb
    out_spec = pl.BlockSpec((None, t, GROUP_DIM), lambda b: (b, 0, 0))
    st_spec = pl.BlockSpec((None, None, kv_dim, lb), lambda b: (layer, b, 0, 0))
    args = [qkv, state_t, bb, bn]
    in_specs = [
        pl.BlockSpec((None, t, QKV_DIM), lambda b: (b, 0, 0)),
        st_spec,
        pl.BlockSpec((rows, lb), lambda b: (0, 0)),
        pl.BlockSpec((rows, DIL_CHUNK), lambda b: (0, 0)),
    ]
    aliases = {}
    if prev_out is not None:
        args.append(prev_out)
        in_specs.append(pl.BlockSpec(memory_space=pl.ANY))
        aliases = {4: 2}
    return pl.pallas_call(
        functools.partial(_dil_sample_kernel, g=g, t_new=t),
        grid=(db,),
        in_specs=in_specs,
        out_specs=[out_spec, out_spec, st_spec],
        out_shape=[jax.ShapeDtypeStruct((db, t, GROUP_DIM), F32)] * 2 + [jax.ShapeDtypeStruct(state_t.shape, F32)],
        input_output_aliases=aliases,
        compiler_params=_cparams(("parallel",)),
        name=f"dil_sample{g}",
    )(*args)


def _dil_sample_bias(bias_d, g, win, dil, lb, t_new):
    z = BIAS_PAD
    assert DIL_CHUNK <= BIAS_PAD and lb + t_new <= BIAS_LEN
    heads = bias_d[g * GROUP_HEADS:(g + 1) * GROUP_HEADS]
    bb, bn = [], []
    pos = np.arange(lb)
    tk = np.arange(DIL_CHUNK)
    for t in range(t_new):
        dist = lb + t - pos
        vals = heads[:, z + t + 1:z + t + 1 + lb][:, ::-1]
        bb.append(jnp.where(jnp.asarray((dist % dil == 0) & (dist <= win)), vals, NEG))
        vals = heads[:, z + t - (DIL_CHUNK - 1):z + t + 1][:, ::-1]
        ok = (tk < t_new) & (t - tk >= 0) & ((t - tk) % dil == 0) & (t - tk <= win)
        bn.append(jnp.where(jnp.asarray(ok), vals, NEG))
    as_rows = lambda tabs: jnp.stack(tabs, axis=1).reshape(GROUP_HEADS * t_new, -1)
    return as_rows(bb), as_rows(bn)


def _oproj_kernel(h_ref, o_ref, gate_ref, whi_ref, wlo_ref, out_ref):
    out_ref[...] = h_ref[...] + gate_ref[...] * _dot3(o_ref[...], whi_ref[...], wlo_ref[...])


def _oproj_dil_kernel(h_ref, o0_ref, o1_ref, o2_ref, l0_ref, l1_ref, l2_ref, gate_ref, whi_ref, wlo_ref, out_ref):
    l0, l1, l2 = l0_ref[...], l1_ref[...], l2_ref[...]
    mx = jnp.maximum(jnp.maximum(l0, l1), l2)
    e0, e1, e2 = jnp.exp(l0 - mx), jnp.exp(l1 - mx), jnp.exp(l2 - mx)
    inv = 1.0 / (e0 + e1 + e2)
    o = jnp.concatenate([o0_ref[...] * (e0 * inv), o1_ref[...] * (e1 * inv), o2_ref[...] * (e2 * inv)], axis=1)
    out_ref[...] = h_ref[...] + gate_ref[...] * _dot3(o, whi_ref[...], wlo_ref[...])


def _oproj(h, attn, gate, w_hi, w_lo, tm, tiles_per_seg):
    n, d = h.shape
    row = lambda w: pl.BlockSpec((tm, w), lambda i: (i, 0))
    if isinstance(attn, tuple):
        kern, attn_args = _oproj_dil_kernel, list(attn[0]) + list(attn[1])
        attn_specs = [row(GROUP_DIM)] * 6
    else:
        kern, attn_args, attn_specs = _oproj_kernel, [attn], [row(ATTN_DIM)]
    return pl.pallas_call(
        kern,
        grid=(n // tm,),
        in_specs=[row(d)] + attn_specs + [
            _mod_spec(gate, tm, tiles_per_seg),
            pl.BlockSpec((ATTN_DIM, d), lambda i: (0, 0)),
            pl.BlockSpec((ATTN_DIM, d), lambda i: (0, 0)),
        ],
        out_specs=row(d),
        out_shape=jax.ShapeDtypeStruct((n, d), F32),
        compiler_params=_cparams(("parallel",)),
        name="oproj",
    )(h, *attn_args, gate, w_hi, w_lo)


def _topk_rows(jobs, ridx, k):
    nrows = ridx.shape[0]

    def body(it, carry):
        for src_ref, val_ref, idx_ref in jobs:
            a = src_ref[0:nrows, :]
            mx = jnp.max(a, axis=0, keepdims=True)
            ix = jnp.min(jnp.where(a == mx, ridx, 1e9), axis=0, keepdims=True)
            src_ref[0:nrows, :] = jnp.where(ridx == ix, -jnp.inf, a)
            val_ref[pl.ds(it, 1), :] = mx
            idx_ref[pl.ds(it, 1), :] = ix
        return carry

    lax.fori_loop(0, k, body, 0)


def _cand_rows(k):
    return [(a, k // (a + 1)) for a in range(k)]


def _peer_select_kernel(x_ref, g_ref, sh_ref, sc_ref, whi_ref, wlo_ref, sk_ref, a_ref, b_ref, gate_ref,
                        qt_ref, s_ref, s2_ref, t1_ref, i1_ref, t2_ref, i2_ref, tc_ref, ic_ref):
    h = pl.program_id(1)
    k = PEER_TOPK
    half = PEER_DK // 2
    tm = x_ref.shape[0]

    @pl.when(h == 0)
    def _():
        f_hi, f_lo = _split(_norm_mod(x_ref[...], g_ref[...], sh_ref[...], sc_ref[...]))
        whi = whi_ref[...]
        qt_ref[...] = _dot_nt(whi, f_hi) + _dot_nt(wlo_ref[...], f_hi) + _dot_nt(whi, f_lo)

    key_rows = lax.broadcasted_iota(jnp.int32, (PEER_NKEYS, tm), 0).astype(F32)
    row0 = pl.multiple_of(h * PEER_DK, PEER_DK)
    for part, sub_ref in enumerate((s_ref, s2_ref)):
        q_hi, q_lo = _split(qt_ref[pl.ds(row0 + part * half, half), :])
        sub_ref[...] = _dot3(sk_ref[part], q_hi, q_lo)
    _topk_rows([(s_ref, t1_ref, i1_ref), (s2_ref, t2_ref, i2_ref)], key_rows, k)
    off, flat = 0, []
    for a, nb in _cand_rows(k):
        s_ref[off:off + nb, :] = t1_ref[a:a + 1, :] + t2_ref[0:nb, :]
        flat.append(lax.broadcasted_iota(jnp.int32, (nb, tm), 0).astype(F32) + float(a * k))
        off += nb
    n_cand = -(-off // 8) * 8
    s_ref[off:n_cand, :] = jnp.full((n_cand - off, tm), -jnp.inf, F32)
    flat.append(jnp.full((n_cand - off, tm), float(k * k), F32))
    _topk_rows([(s_ref, tc_ref, ic_ref)], jnp.concatenate(flat, axis=0), k)
    jc = ic_ref[...]
    ja = jnp.floor(jc * (1.0 / k))
    jb = jc - ja * k
    e1 = jnp.zeros(jc.shape, F32)
    e2 = jnp.zeros(jc.shape, F32)
    for a in range(k):
        e1 = e1 + jnp.where(ja == float(a), i1_ref[a:a + 1, :], 0.0)
        e2 = e2 + jnp.where(jb == float(a), i2_ref[a:a + 1, :], 0.0)
    top = tc_ref[...]
    e = jnp.exp(top - top[0:1, :])
    a_ref[...] = e1
    b_ref[...] = e2
    gate_ref[...] = e / jnp.sum(e, axis=0, keepdims=True)


def _peer_select(x, g, shift, scale, wqt_hi, wqt_lo, sub_keys, tm, tiles_per_seg):
    n, d = x.shape
    k = PEER_TOPK
    nq = PEER_HEADS * PEER_DK
    out_spec = pl.BlockSpec((k, tm), lambda i, h: (h, i))
    out_shape = jax.ShapeDtypeStruct((PEER_HEADS * k, n), F32)
    mod_spec = lambda mod: pl.BlockSpec((None, mod.shape[1], d), lambda i, h: (i // tiles_per_seg, 0, 0))
    return pl.pallas_call(
        _peer_select_kernel,
        grid=(n // tm, PEER_HEADS),
        in_specs=[
            pl.BlockSpec((tm, d), lambda i, h: (i, 0)),
            pl.BlockSpec((1, d), lambda i, h: (0, 0)),
            mod_spec(shift),
            mod_spec(scale),
            pl.BlockSpec((nq, d), lambda i, h: (0, 0)),
            pl.BlockSpec((nq, d), lambda i, h: (0, 0)),
            pl.BlockSpec((2, PEER_NKEYS, PEER_DK // 2), lambda i, h: (0, 0, 0)),
        ],
        out_specs=[out_spec] * 3,
        out_shape=[out_shape] * 3,
        scratch_shapes=[pltpu.VMEM((nq, tm), F32)] + [pltpu.VMEM((PEER_NKEYS, tm), F32)] * 2
        + [pltpu.VMEM((k, tm), F32)] * 6,
        compiler_params=_cparams(("parallel", "arbitrary")),
        name="peer_select",
    )(x, g.reshape(1, d), shift, scale, wqt_hi, wqt_lo, sub_keys)


def _peer_dense_kernel(x_ref, g_ref, sh_ref, sc_ref, gt_ref, a_ref, b_ref, w_ref, u_ref, v_ref, out_ref,
                       f_ref, gs_ref, at_ref, bt_ref, wt_ref, p_ref, acc_ref):
    e = pl.program_id(1)
    tc = x_ref.shape[0]
    te = u_ref.shape[0]
    nk = PEER_NKEYS
    half = nk // 2
    npair = te // (2 * nk)
    hi_mask = jnp.int32(-65536)

    @pl.when(e == 0)
    def _():
        f_ref[...] = _norm_mod(x_ref[...], g_ref[...], sh_ref[...], sc_ref[...]).astype(BF16)
        acc_ref[...] = jnp.zeros(acc_ref.shape, F32)
        at_ref[...] = a_ref[...].T
        bt_ref[...] = b_ref[...].T
        wt_ref[...] = w_ref[...].T
        key = lax.broadcasted_iota(jnp.int32, (nk, nk), 0).astype(F32)

        def build(c, carry):
            first = jnp.where(key == at_ref[pl.ds(c, 1), :], 1.0, 0.0).astype(BF16)
            second = jnp.where(key == bt_ref[pl.ds(c, 1), :], wt_ref[pl.ds(c, 1), :], 0.0).astype(BF16)
            g = pltpu.bitcast(_dot_nt(first, second), jnp.int32) + 0x8000
            packed = (g[:half] & hi_mask) | lax.shift_right_logical(g[half:], jnp.full((half, nk), 16, jnp.int32))
            for k in range(half // 8):
                gs_ref[pl.ds(pl.multiple_of((k * tc + c) * 8, 8), 8), :] = packed[k * 8:(k + 1) * 8, :]
            return carry

        lax.fori_loop(0, tc, build, 0, unroll=16)

    for r in range(npair):
        p = e * npair + r
        packed = gs_ref[pl.ds((p // 8) * (tc * 8) + p % 8, tc, stride=8), :]
        g_a = pltpu.bitcast(packed & hi_mask, F32)
        g_b = pltpu.bitcast(packed << 16, F32)
        act = _dot_nt(f_ref[...], u_ref[r * 2 * nk:(r + 1) * 2 * nk, :])
        gelu = 0.5 * act * (1.0 + lax.erf(act * (2.0 ** -0.5)))
        p_ref[:, r * 2 * nk:r * 2 * nk + nk] = (g_a * gelu[:, :nk]).astype(BF16)
        p_ref[:, r * 2 * nk + nk:(r + 1) * 2 * nk] = (g_b * gelu[:, nk:]).astype(BF16)
    acc_ref[...] += _dot(p_ref[...], v_ref[...])

    @pl.when(e == pl.num_programs(1) - 1)
    def _():
        out_ref[...] = x_ref[...] + gt_ref[...] * acc_ref[...]


def _pair_permute(tab):
    half = PEER_NKEYS // 2
    t = tab.reshape(tab.shape[:-2] + (2, half, PEER_NKEYS, tab.shape[-1]))
    return jnp.swapaxes(t, -4, -3).reshape(tab.shape).astype(BF16)


def _peer_dense(x, g, shift, scale, gate, first, second, weight, u_tab, v_tab, tc, te, tiles_per_seg):
    n, d = x.shape
    n_exp = u_tab.shape[0]
    nhk = PEER_HEADS * PEER_TOPK
    mod_spec = lambda mod: pl.BlockSpec((None, mod.shape[1], d), lambda i, e: (i // tiles_per_seg, 0, 0))
    sel_spec = pl.BlockSpec((nhk, tc), lambda i, e: (0, i))
    return pl.pallas_call(
        _peer_dense_kernel,
        grid=(n // tc, n_exp // te),
        in_specs=[
            pl.BlockSpec((tc, d), lambda i, e: (i, 0)),
            pl.BlockSpec((1, d), lambda i, e: (0, 0)),
            mod_spec(shift), mod_spec(scale), mod_spec(gate),
            sel_spec, sel_spec, sel_spec,
            pl.BlockSpec((te, d), lambda i, e: (e, 0)),
            pl.BlockSpec((te, d), lambda i, e: (e, 0)),
        ],
        out_specs=pl.BlockSpec((tc, d), lambda i, e: (i, 0)),
        out_shape=jax.ShapeDtypeStruct((n, d), F32),
        scratch_shapes=[
            pltpu.VMEM((tc, d), BF16),
            pltpu.VMEM((tc * PEER_NKEYS // 2, PEER_NKEYS), jnp.int32),
            pltpu.VMEM((tc, nhk), F32),
            pltpu.VMEM((tc, nhk), F32),
            pltpu.VMEM((tc, nhk), F32),
            pltpu.VMEM((tc, te), BF16),
            pltpu.VMEM((tc, d), F32),
        ],
        compiler_params=_cparams(("parallel", "arbitrary")),
        name="peer_dense",
    )(x, g.reshape(1, d), shift, scale, gate, first, second, weight, u_tab, v_tab)


def _final_norm_kernel(x_ref, g_ref, o_ref):
    x = x_ref[...]
    ms = jnp.mean(x * x, axis=-1, keepdims=True)
    o_ref[...] = x * lax.rsqrt(ms + RMS_EPS) * g_ref[...]


def _final_norm(x, g, tm):
    n, d = x.shape
    return pl.pallas_call(
        _final_norm_kernel,
        grid=(n // tm,),
        in_specs=[pl.BlockSpec((tm, d), lambda i: (i, 0)), pl.BlockSpec((1, d), lambda i: (0, 0))],
        out_specs=pl.BlockSpec((tm, d), lambda i: (i, 0)),
        out_shape=jax.ShapeDtypeStruct((n, d), F32),
        compiler_params=_cparams(("parallel",)),
        name="final_norm",
    )(x, g.reshape(1, d))


TM = 256
PEER_TC = 512
PEER_TE = 2048


def _split_w(w):
    hi = w.astype(BF16)
    return hi, (w - hi.astype(F32)).astype(BF16)


def kernel(x_prompt, x_sample, cache_k_moba, cache_v_moba, state_kv_dil0, state_kv_dil1, state_kv_dil2, page_table,
           c_prompt, c_sample, w_ada, b_ada, g_norm, w_qkv, w_o, rel_bias, w_pq, peer_sub_keys, peer_u, peer_v,
           g_final):
    B, S, D = x_prompt.shape
    DB, T, _ = x_sample.shape
    n_a, n_pool = cache_k_moba.shape[:2]
    n_pages = page_table.shape[1]
    past_len = n_pages * PAGE_SIZE
    assert D == D_MODEL and S % MOBA_BLOCK == 0 and past_len % MOBA_BLOCK == 0 and S % TM == 0
    dil_states = (state_kv_dil0, state_kv_dil1, state_kv_dil2)
    for st, (win, dil) in zip(dil_states, DIL_GROUPS):
        assert st.shape[2] == win and win // dil == DIL_CHUNK and S % (dil * DIL_CHUNK) == 0

    np_, ns_ = B * S, DB * T
    tms = min(TM, ns_)
    hp = x_prompt.reshape(np_, D)
    hs = x_sample.reshape(ns_, D)

    mod_all = _ada(jnp.concatenate([c_prompt, c_sample], axis=0), w_ada, b_ada)
    wqkv_hi, wqkv_lo = _split_w(w_qkv)
    wo_hi, wo_lo = _split_w(w_o)
    wpqt_hi, wpqt_lo = _split_w(jnp.swapaxes(w_pq, 1, 2))
    u_bf, v_bf = _pair_permute(peer_u), _pair_permute(peer_v)
    pool_kt = cache_k_moba.transpose(0, 1, 3, 4, 2).reshape(n_a, n_pool, ATTN_DIM, PAGE_SIZE)
    pool_vt = cache_v_moba.transpose(0, 1, 3, 4, 2).reshape(n_a, n_pool, ATTN_DIM, PAGE_SIZE)

    bias_d = _bias_by_distance(rel_bias)
    moba_bias_p = _moba_prompt_bias(bias_d, S)
    moba_bias_s = _moba_sample_bias(bias_d, past_len, T)
    dil_bias_p = [_dil_prompt_bias(bias_d, g, w, d) for g, (w, d) in enumerate(DIL_GROUPS)]
    dil_bias_s = [_dil_sample_bias(bias_d, g, w, d, dil_states[g].shape[2], T)
                  for g, (w, d) in enumerate(DIL_GROUPS)]

    states_t = [st.transpose(0, 1, 3, 4, 5, 2).reshape(st.shape[0], DB, 2 * GROUP_DIM, st.shape[2])
                for st in dil_states]

    kp_rows, vp_rows, ks_rows, vs_rows = [], [], [], []
    dil_p = [[] for _ in DIL_GROUPS]
    dil_s = [None for _ in DIL_GROUPS]
    for i in range(DEPTH):
        j = i // 2
        mod = mod_all[i].reshape(B + DB, 6, D)
        mp = [mod[:B, c].reshape(B, 1, D) for c in range(6)]
        ms = [jnp.repeat(mod[B:, c], T, axis=0).reshape(ns_ // tms, tms, D) for c in range(6)]
        seg_p, seg_s = S // TM, 1

        qkv_p = _nmm(hp, g_norm[i, 0], mp[0], mp[1], wqkv_hi[i], wqkv_lo[i], TM, seg_p)
        qkv_s = _nmm(hs, g_norm[i, 0], ms[0], ms[1], wqkv_hi[i], wqkv_lo[i], tms, seg_s)
        qkv_p3 = qkv_p.reshape(B, S, QKV_DIM)
        qkv_s3 = qkv_s.reshape(DB, T, QKV_DIM)
        heads = lambda a, n, l: a.reshape(n, l, 3, N_HEADS, HEAD_DIM)
        qkv_p5, qkv_s5 = heads(qkv_p, B, S), heads(qkv_s, DB, T)
        if i % 2 == 0:
            attn_p = _moba_prompt(qkv_p3, moba_bias_p).reshape(np_, ATTN_DIM)
            attn_s = _moba_sample(qkv_s3, pool_kt, pool_vt, page_table, moba_bias_s, j, n_pages).reshape(ns_, ATTN_DIM)
            kp_rows.append(qkv_p5[:, :, 1])
            vp_rows.append(qkv_p5[:, :, 2])
            ks_rows.append(qkv_s5[:, :, 1])
            vs_rows.append(qkv_s5[:, :, 2])
        else:
            op, lp, os_, ls = [], [], [], []
            for g, (win, dil) in enumerate(DIL_GROUPS):
                o, lse = _dil_prompt(qkv_p3, dil_bias_p[g], g, dil)
                op.append(o.reshape(np_, GROUP_DIM))
                lp.append(lse.reshape(np_, GROUP_DIM))
                o, lse, dil_s[g] = _dil_sample(qkv_s3, states_t[g], dil_s[g], *dil_bias_s[g], j, g)
                os_.append(o.reshape(ns_, GROUP_DIM))
                ls.append(lse.reshape(ns_, GROUP_DIM))
                hsl = slice(g * GROUP_HEADS, (g + 1) * GROUP_HEADS)
                kv_p = jnp.stack([qkv_p5[:, :, 1, hsl], qkv_p5[:, :, 2, hsl]], axis=2)
                dil_p[g].append(kv_p[:, S - min(win, S):])
            attn_p, attn_s = (op, lp), (os_, ls)
        hp = _oproj(hp, attn_p, mp[2], wo_hi[i], wo_lo[i], TM, seg_p)
        hs = _oproj(hs, attn_s, ms[2], wo_hi[i], wo_lo[i], tms, seg_s)

        sel_p = _peer_select(hp, g_norm[i, 1], mp[3], mp[4], wpqt_hi[i], wpqt_lo[i], peer_sub_keys[i], TM, seg_p)
        sel_s = _peer_select(hs, g_norm[i, 1], ms[3], ms[4], wpqt_hi[i], wpqt_lo[i], peer_sub_keys[i], tms, seg_s)
        hp = _peer_dense(hp, g_norm[i, 1], mp[3], mp[4], mp[5], *sel_p, u_bf[i], v_bf[i], PEER_TC, PEER_TE,
                         S // PEER_TC)
        hs = _peer_dense(hs, g_norm[i, 1], ms[3], ms[4], ms[5], *sel_s, u_bf[i], v_bf[i], tms, PEER_TE, 1)

    y_prompt = _final_norm(hp, g_final, TM).reshape(B, S, D)
    y_sample = _final_norm(hs, g_final, tms).reshape(DB, T, D)
    return (y_prompt, y_sample,
            jnp.stack(kp_rows), jnp.stack(vp_rows), jnp.stack(ks_rows), jnp.stack(vs_rows),
            jnp.stack(dil_p[0]), jnp.stack(dil_p[1]), jnp.stack(dil_p[2]),
            *[st.reshape(st.shape[:2] + (2, GROUP_HEADS, HEAD_DIM, st.shape[3])).transpose(0, 1, 5, 2, 3, 4)
              for st in dil_s])
```

```python
import functools
import math

import numpy as np
import jax
import jax.numpy as jnp
from jax import lax
from jax.experimental import pallas as pl
from jax.experimental.pallas import tpu as pltpu

F32 = jnp.float32
BF16 = jnp.bfloat16
NEG = -1e30

D_MODEL = 1024
N_HEADS = 12
HEAD_DIM = 64
ATTN_DIM = N_HEADS * HEAD_DIM
QKV_DIM = 3 * ATTN_DIM
ATTN_SCALE = HEAD_DIM ** -0.5
DEPTH = 4
PAGE_SIZE = 128
MOBA_BLOCK = 256
MOBA_TOPK = 3
DIL_GROUPS = ((128, 1), (512, 4), (2048, 16))
GROUP_HEADS = 4
GROUP_DIM = GROUP_HEADS * HEAD_DIM
DIL_CHUNK = 128
REL_BUCKETS = 32
REL_MAX_DIST = 2048
PEER_HEADS = 8
PEER_NKEYS = 128
PEER_DK = 256
PEER_TOPK = 16
RMS_EPS = 1e-6

VMEM_LIMIT = 56 * 1024 * 1024


def _cparams(sem):
    return pltpu.CompilerParams(dimension_semantics=sem, vmem_limit_bytes=VMEM_LIMIT)


def _split(x):
    hi = x.astype(BF16)
    lo = (x - hi.astype(F32)).astype(BF16)
    return hi, lo


def _dot(a, b):
    return jnp.dot(a, b, preferred_element_type=F32)


def _dot_nt(a, b):
    return lax.dot_general(a, b, (((1,), (1,)), ((), ())), preferred_element_type=F32)


def _dot3(a, b_hi, b_lo):
    a_hi, a_lo = _split(a)
    return _dot(a_hi, b_hi) + _dot(a_lo, b_hi) + _dot(a_hi, b_lo)


def _dot3_nt(a, b):
    a_hi, a_lo = _split(a)
    b_hi, b_lo = _split(b)
    return _dot_nt(a_hi, b_hi) + _dot_nt(a_lo, b_hi) + _dot_nt(a_hi, b_lo)


def _norm_mod(x, g, shift, scale):
    ms = jnp.mean(x * x, axis=-1, keepdims=True)
    xn = x * lax.rsqrt(ms + RMS_EPS) * g
    return xn * (1.0 + scale) + shift


def _bucket_np(dist):
    n = np.maximum(dist, 0)
    max_exact = REL_BUCKETS // 2
    nf = np.maximum(n, 1).astype(np.float64)
    large = max_exact + (np.log(nf / max_exact) / math.log(REL_MAX_DIST / max_exact)
                         * (REL_BUCKETS - max_exact)).astype(np.int64)
    large = np.minimum(large, REL_BUCKETS - 1)
    return np.where(n < max_exact, n, large).astype(np.int32)


BIAS_PAD = 256
BIAS_LEN = 4352


def _bias_by_distance(rel_bias):
    onehot = np.zeros((BIAS_LEN, REL_BUCKETS), np.float32)
    onehot[np.arange(BIAS_LEN), _bucket_np(np.arange(BIAS_LEN))] = 1.0
    by_dist = jnp.dot(jnp.asarray(onehot), rel_bias, precision=lax.Precision.HIGHEST)
    return jnp.concatenate([jnp.full((N_HEADS, BIAS_PAD), NEG, F32), by_dist.T], axis=1)


def _toeplitz(wp, rows, cols):
    n2 = wp.shape[-1]
    x = jnp.tile(wp, (1,) * (wp.ndim - 1) + (rows,))[..., :rows * (n2 - 1)]
    return x.reshape(wp.shape[:-1] + (rows, n2 - 1))[..., :cols]


def _rank_select(s, lane_n, n_valid, topk):
    nb = s.shape[1]
    rank = jnp.zeros(s.shape, F32)
    for m in range(nb):
        sm = s[:, m:m + 1]
        beats = (sm > s) | ((sm == s) & (m < lane_n))
        beats = beats & (m < n_valid)
        rank = rank + jnp.where(beats, 1.0, 0.0)
    sel = (lane_n < n_valid) & (rank < float(topk))
    return jnp.where(sel, 1.0, 0.0)


def _rank_select_rows(s, row_n, n_valid, topk):
    nb = s.shape[0]
    rank = jnp.zeros(s.shape, F32)
    for m in range(nb):
        sm = s[m:m + 1, :]
        beats = (sm > s) | ((sm == s) & (m < row_n))
        beats = beats & (m < n_valid)
        rank = rank + jnp.where(beats, 1.0, 0.0)
    sel = (row_n < n_valid) & (rank < float(topk))
    return jnp.where(sel, 1.0, 0.0)


def _ada_kernel(c_ref, w_ref, b_ref, o_ref):
    c = c_ref[...]
    s = c * (1.0 / (1.0 + jnp.exp(-c)))
    w_hi, w_lo = _split(w_ref[...])
    o_ref[...] = _dot3(s, w_hi, w_lo) + b_ref[...]


def _ada(c_all, w_ada, b_ada):
    n = c_all.shape[0]
    depth, d, n6 = w_ada.shape
    tn = 1536
    return pl.pallas_call(
        _ada_kernel,
        grid=(depth, n6 // tn),
        in_specs=[
            pl.BlockSpec((n, d), lambda i, j: (0, 0)),
            pl.BlockSpec((None, d, tn), lambda i, j: (i, 0, j)),
            pl.BlockSpec((None, 1, tn), lambda i, j: (i, 0, j)),
        ],
        out_specs=pl.BlockSpec((None, n, tn), lambda i, j: (i, 0, j)),
        out_shape=jax.ShapeDtypeStruct((depth, n, n6), F32),
        compiler_params=_cparams(("parallel", "parallel")),
        name="ada",
    )(c_all, w_ada, b_ada.reshape(depth, 1, n6))


def _nmm_kernel(x_ref, g_ref, sh_ref, sc_ref, w_ref, o_ref):
    f = _norm_mod(x_ref[...], g_ref[...], sh_ref[...], sc_ref[...])
    o_ref[...] = _dot(f.astype(BF16), w_ref[...])


def _mod_spec(mod, tm, tiles_per_seg):
    r = mod.shape[1]
    return pl.BlockSpec((None, r, D_MODEL), lambda i, *_: (i // tiles_per_seg, 0, 0))


def _nmm(x, g, shift, scale, w, tm, tiles_per_seg):
    n, d = x.shape
    nout = w.shape[1]
    return pl.pallas_call(
        _nmm_kernel,
        grid=(n // tm,),
        in_specs=[
            pl.BlockSpec((tm, d), lambda i: (i, 0)),
            pl.BlockSpec((1, d), lambda i: (0, 0)),
            _mod_spec(shift, tm, tiles_per_seg),
            _mod_spec(scale, tm, tiles_per_seg),
            pl.BlockSpec((d, nout), lambda i: (0, 0)),
        ],
        out_specs=pl.BlockSpec((tm, nout), lambda i: (i, 0)),
        out_shape=jax.ShapeDtypeStruct((n, nout), F32),
        compiler_params=_cparams(("parallel",)),
        name="nmm",
    )(x, g.reshape(1, d), shift, scale, w)


def _moba_prompt_kernel(q_ref, k_ref, v_ref, bias_ref, o_ref, kmean_ref, mask_ref):
    i = pl.program_id(2)
    blk = MOBA_BLOCK
    nb = k_ref.shape[0] // blk

    @pl.when(i == 0)
    def _():
        for n in range(nb):
            kmean_ref[n:n + 1, :] = jnp.mean(k_ref[n * blk:(n + 1) * blk, :], axis=0, keepdims=True)

    lane_head = lax.broadcasted_iota(jnp.int32, (blk, 128), 1) // HEAD_DIM
    row_n = lax.broadcasted_iota(jnp.int32, (nb, blk), 0)
    km_head = lax.broadcasted_iota(jnp.int32, (nb, 128), 1) // HEAD_DIM
    q2 = q_ref[...] * ATTN_SCALE
    qb = []
    for hh in range(2):
        q = jnp.where(lane_head == hh, q2, 0.0)
        scores = _dot3_nt(jnp.where(km_head == hh, kmean_ref[...], 0.0), q)
        sel = _rank_select_rows(scores, row_n, i, MOBA_TOPK)
        sel = jnp.concatenate([sel, jnp.zeros((128 - nb, blk), F32)], axis=0).T
        for n in range(nb):
            mask_ref[hh, n] = jnp.broadcast_to(sel[:, n:n + 1], (blk, 128))
        qb.append(q.astype(BF16))

    def tile(j, hh, masked):
        r0 = pl.multiple_of(j * blk, blk)
        logits = _dot_nt(qb[hh], k_ref[pl.ds(r0, blk), :].astype(BF16)) + bias_ref[i - j, hh]
        if masked:
            keep = mask_ref[hh, j] > 0.5
            logits = jnp.where(jnp.concatenate([keep] * (blk // 128), axis=1), logits, NEG)
        return logits, v_ref[pl.ds(r0, blk), :].astype(BF16)

    state = []
    for hh in range(2):
        logits, vb = tile(i, hh, False)
        m = jnp.max(logits, axis=1, keepdims=True)
        p = jnp.exp(logits - m)
        state += [m, jnp.sum(p, axis=1, keepdims=True), _dot(p.astype(BF16), vb)]

    def body(u, carry):
        out = []
        for hh in range(2):
            m, l, acc = carry[3 * hh:3 * hh + 3]
            lg0, v0 = tile(2 * u, hh, True)
            lg1, v1 = tile(2 * u + 1, hh, True)
            m_new = jnp.maximum(m, jnp.maximum(jnp.max(lg0, axis=1, keepdims=True), jnp.max(lg1, axis=1, keepdims=True)))
            alpha = jnp.exp(m - m_new)
            p0 = jnp.exp(lg0 - m_new)
            p1 = jnp.exp(lg1 - m_new)
            l = alpha * l + jnp.sum(p0, axis=1, keepdims=True) + jnp.sum(p1, axis=1, keepdims=True)
            acc = alpha * acc + _dot(p0.astype(BF16), v0) + _dot(p1.astype(BF16), v1)
            out += [m_new, l, acc]
        return tuple(out)

    _, l0, acc0, _, l1, acc1 = lax.fori_loop(0, (i + 1) // 2, body, tuple(state))
    o_ref[...] = jnp.where(lane_head == 0, acc0 / l0, acc1 / l1)


def _moba_prompt(qkv, bias):
    b, s, _ = qkv.shape
    blk = MOBA_BLOCK
    nb = s // blk
    npair = N_HEADS // 2
    return pl.pallas_call(
        _moba_prompt_kernel,
        grid=(npair, b, nb),
        in_specs=[
            pl.BlockSpec((None, blk, 128), lambda hp, bb, i: (bb, i, hp)),
            pl.BlockSpec((None, s, 128), lambda hp, bb, i: (bb, 0, npair + hp)),
            pl.BlockSpec((None, s, 128), lambda hp, bb, i: (bb, 0, 2 * npair + hp)),
            pl.BlockSpec((nb, 2, blk, blk), lambda hp, bb, i: (0, hp, 0, 0)),
        ],
        out_specs=pl.BlockSpec((None, blk, 128), lambda hp, bb, i: (bb, i, hp)),
        out_shape=jax.ShapeDtypeStruct((b, s, ATTN_DIM), F32),
        scratch_shapes=[pltpu.VMEM((nb, 128), F32), pltpu.VMEM((2, nb, blk, 128), F32)],
        compiler_params=_cparams(("parallel", "parallel", "arbitrary")),
        name="moba_prompt",
    )(qkv, qkv, qkv, bias)


def _moba_prompt_bias(bias_d, s):
    blk = MOBA_BLOCK
    nb = s // blk
    assert blk <= BIAS_PAD
    wps = []
    for delta in range(nb):
        z = BIAS_PAD + delta * blk
        lo = bias_d[:, z - blk:z + 1][:, ::-1]
        hi = bias_d[:, z + 1:z + blk][:, ::-1]
        wps.append(jnp.concatenate([lo, hi], axis=1))
    return _toeplitz(jnp.stack(wps), blk, blk)


def _moba_sample_kernel(pt_ref, new_ref, *refs, n_pages, t_new):
    del pt_ref
    kt_refs, vt_refs = refs[:n_pages], refs[n_pages:2 * n_pages]
    sbias_ref, o_ref, lg_ref, p_ref = refs[2 * n_pages:]
    rows = N_HEADS * t_new
    pages_per_block = MOBA_BLOCK // PAGE_SIZE
    n_blocks = n_pages // pages_per_block
    r_head = lax.broadcasted_iota(jnp.int32, (rows, ATTN_DIM), 0) // t_new
    c_head = lax.broadcasted_iota(jnp.int32, (rows, ATTN_DIM), 1) // HEAD_DIM
    pad = jnp.zeros((PAGE_SIZE - t_new, ATTN_DIM), F32)

    q = new_ref[:, 0:ATTN_DIM] * ATTN_SCALE
    qhi, qlo = _split(jnp.where(r_head == c_head, jnp.concatenate([q] * N_HEADS, axis=0), 0.0))

    scores = []
    for n in range(n_blocks):
        ksum = None
        for pg in range(n * pages_per_block, (n + 1) * pages_per_block):
            kt = kt_refs[pg][...]
            lg_ref[pg] = _dot(qhi, kt.astype(BF16))
            ksum = kt if ksum is None else ksum + kt
        k_hi, k_lo = _split(ksum)
        y = _dot(qhi, k_hi) + _dot(qlo, k_hi) + _dot(qhi, k_lo)
        scores.append(jnp.sum(y, axis=1, keepdims=True))
    s = jnp.concatenate(scores, axis=1)
    lane_n = lax.broadcasted_iota(jnp.int32, (rows, n_blocks), 1)
    sel = _rank_select(s, lane_n, n_blocks, MOBA_TOPK)
    knew = jnp.concatenate([new_ref[:, ATTN_DIM:2 * ATTN_DIM], pad], axis=0)
    lg_ref[n_pages] = _dot_nt(qhi, knew.astype(BF16))

    mrun = jnp.full((rows, PAGE_SIZE), NEG, F32)
    for pg in range(n_pages + 1):
        lgt = lg_ref[pg] + sbias_ref[pg]
        if pg < n_pages:
            n = pg // pages_per_block
            lgt = jnp.where(sel[:, n:n + 1] > 0.5, lgt, NEG)
        lg_ref[pg] = lgt
        mrun = jnp.maximum(mrun, lgt)
    m = jnp.max(mrun, axis=1, keepdims=True)
    srun = jnp.zeros((rows, PAGE_SIZE), F32)
    for pg in range(n_pages + 1):
        e = jnp.exp(lg_ref[pg] - m)
        srun = srun + e
        p_ref[pg] = e.astype(BF16)
    linv = 1.0 / jnp.sum(srun, axis=1, keepdims=True)

    vnew = jnp.concatenate([new_ref[:, 2 * ATTN_DIM:3 * ATTN_DIM], pad], axis=0)
    acc = _dot(p_ref[n_pages], vnew.astype(BF16))
    for pg in range(n_pages):
        acc = acc + _dot_nt(p_ref[pg], vt_refs[pg][...].astype(BF16))
    acc = jnp.where(r_head == c_head, acc * linv, 0.0)
    out = acc[0:t_new, :]
    for h in range(1, N_HEADS):
        out = out + acc[h * t_new:(h + 1) * t_new, :]
    o_ref[...] = out


def _moba_sample(qkv, pool_kt, pool_vt, page_table, sbias, layer, n_pages):
    db, t, _ = qkv.shape
    rows = N_HEADS * t
    page_spec = lambda pg: pl.BlockSpec((None, None, ATTN_DIM, PAGE_SIZE), lambda b, pt: (layer, pt[b, pg], 0, 0))
    grid_spec = pltpu.PrefetchScalarGridSpec(
        num_scalar_prefetch=1,
        grid=(db,),
        in_specs=[pl.BlockSpec((None, t, QKV_DIM), lambda b, pt: (b, 0, 0))]
        + [page_spec(pg) for pg in range(n_pages)] * 2
        + [pl.BlockSpec((n_pages + 1, rows, PAGE_SIZE), lambda b, pt: (0, 0, 0))],
        out_specs=pl.BlockSpec((None, t, ATTN_DIM), lambda b, pt: (b, 0, 0)),
        scratch_shapes=[
            pltpu.VMEM((n_pages + 1, rows, PAGE_SIZE), F32),
            pltpu.VMEM((n_pages + 1, rows, PAGE_SIZE), BF16),
        ],
    )
    return pl.pallas_call(
        functools.partial(_moba_sample_kernel, n_pages=n_pages, t_new=t),
        grid_spec=grid_spec,
        out_shape=jax.ShapeDtypeStruct((db, t, ATTN_DIM), F32),
        compiler_params=_cparams(("parallel",)),
        name="moba_sample",
    )(page_table, qkv, *([pool_kt] * n_pages), *([pool_vt] * n_pages), sbias)


def _moba_sample_bias(bias_d, past_len, t):
    n_pages = past_len // PAGE_SIZE
    z = BIAS_PAD
    assert PAGE_SIZE <= BIAS_PAD and past_len + t <= BIAS_LEN
    past = jnp.stack([bias_d[:, z + tq + 1:z + tq + 1 + past_len][:, ::-1] for tq in range(t)], axis=1)
    past = past.reshape(N_HEADS * t, n_pages, PAGE_SIZE).transpose(1, 0, 2)
    new = jnp.stack([bias_d[:, z + tq - (PAGE_SIZE - 1):z + tq + 1][:, ::-1] for tq in range(t)], axis=1)
    return jnp.concatenate([past, new.reshape(1, N_HEADS * t, PAGE_SIZE)], axis=0)


def _dil_prompt_kernel(q_ref, kp_ref, kc_ref, vp_ref, vc_ref, bias_ref, o_ref, lse_ref):
    c = pl.program_id(2)
    ch = DIL_CHUNK
    col = lax.broadcasted_iota(jnp.int32, (ch, 2 * ch), 1)
    keep = (col >= ch) | (c > 0)
    for h in range(GROUP_HEADS):
        sl = slice(h * HEAD_DIM, (h + 1) * HEAD_DIM)
        q = (q_ref[:, sl] * ATTN_SCALE).astype(BF16)
        k = jnp.concatenate([kp_ref[:, sl], kc_ref[:, sl]], axis=0).astype(BF16)
        v = jnp.concatenate([vp_ref[:, sl], vc_ref[:, sl]], axis=0).astype(BF16)
        s = jnp.where(keep, _dot_nt(q, k) + bias_ref[h], NEG)
        m = jnp.max(s, axis=1, keepdims=True)
        p = jnp.exp(s - m)
        l = jnp.sum(p, axis=1, keepdims=True)
        o_ref[:, sl] = _dot(p.astype(BF16), v) / l
        lse_ref[:, sl] = jnp.broadcast_to(m + jnp.log(l), (ch, HEAD_DIM))


def _dil_prompt(qkv, bias, g, dil):
    b, s, _ = qkv.shape
    ch = DIL_CHUNK
    sub = s // dil
    nch = sub // ch
    nblk = QKV_DIM // GROUP_DIM
    view = qkv.reshape(b, sub, dil * QKV_DIM)
    kcol = ATTN_DIM // GROUP_DIM + g
    vcol = 2 * ATTN_DIM // GROUP_DIM + g
    spec = lambda colblk, prev: pl.BlockSpec(
        (None, ch, GROUP_DIM),
        (lambda bb, r, c: (bb, jnp.maximum(c - 1, 0), r * nblk + colblk)) if prev
        else (lambda bb, r, c: (bb, c, r * nblk + colblk)))
    out_spec = pl.BlockSpec((None, ch, GROUP_DIM), lambda bb, r, c: (bb, c, r))
    o, lse = pl.pallas_call(
        _dil_prompt_kernel,
        grid=(b, dil, nch),
        in_specs=[spec(g, False), spec(kcol, True), spec(kcol, False), spec(vcol, True), spec(vcol, False),
                  pl.BlockSpec((GROUP_HEADS, ch, 2 * ch), lambda bb, r, c: (0, 0, 0))],
        out_specs=[out_spec, out_spec],
        out_shape=[jax.ShapeDtypeStruct((b, sub, dil * GROUP_DIM), F32)] * 2,
        compiler_params=_cparams(("parallel", "parallel", "arbitrary")),
        name=f"dil_prompt{g}",
    )(view, view, view, view, view, bias)
    return o.reshape(b, s, GROUP_DIM), lse.reshape(b, s, GROUP_DIM)


def _dil_prompt_bias(bias_d, g, win, dil):
    ch = DIL_CHUNK
    assert win // dil == ch
    z = BIAS_PAD
    heads = bias_d[g * GROUP_HEADS:(g + 1) * GROUP_HEADS]
    steps = heads[:, z:z + win + 1:dil][:, ::-1]
    wp = jnp.concatenate([steps, jnp.full((GROUP_HEADS, 4 * ch - (ch + 1)), NEG, F32)], axis=1)
    return _toeplitz(wp, ch, 2 * ch)


def _dil_sample_kernel(new_ref, buf_ref, bb_ref, bn_ref, *rest, g, t_new):
    o_ref, lse_ref, st_ref = rest[-3:]
    lb = buf_ref.shape[1]
    rows = GROUP_HEADS * t_new
    ch = DIL_CHUNK
    r_head = lax.broadcasted_iota(jnp.int32, (rows, GROUP_DIM), 0) // t_new
    c_head = lax.broadcasted_iota(jnp.int32, (rows, GROUP_DIM), 1) // HEAD_DIM
    diag = r_head == c_head
    pad = jnp.zeros((ch - t_new, GROUP_DIM), F32)
    qcol = g * GROUP_DIM
    kcol = ATTN_DIM + g * GROUP_DIM
    vcol = 2 * ATTN_DIM + g * GROUP_DIM
    knew, vnew = new_ref[:, kcol:kcol + GROUP_DIM], new_ref[:, vcol:vcol + GROUP_DIM]
    q = new_ref[:, qcol:qcol + GROUP_DIM] * ATTN_SCALE
    qbd = jnp.where(diag, jnp.concatenate([q] * GROUP_HEADS, axis=0), 0.0).astype(BF16)
    buf = buf_ref[...]
    kt, vt = buf[:GROUP_DIM].astype(BF16), buf[GROUP_DIM:].astype(BF16)
    lgb = _dot(qbd, kt) + bb_ref[...]
    lgn = _dot_nt(qbd, jnp.concatenate([knew, pad], axis=0).astype(BF16)) + bn_ref[...]
    m = jnp.maximum(jnp.max(lgb, axis=1, keepdims=True), jnp.max(lgn, axis=1, keepdims=True))
    pb = jnp.exp(lgb - m)
    pn = jnp.exp(lgn - m)
    l = jnp.sum(pb, axis=1, keepdims=True) + jnp.sum(pn, axis=1, keepdims=True)
    o = _dot_nt(pb.astype(BF16), vt) + _dot(pn.astype(BF16), jnp.concatenate([vnew, pad], axis=0).astype(BF16))
    o = jnp.where(diag, o / l, 0.0)
    lse = jnp.where(diag, m + jnp.log(l), 0.0)
    o_ref[...] = sum(o[h * t_new:(h + 1) * t_new] for h in range(GROUP_HEADS))
    lse_ref[...] = sum(lse[h * t_new:(h + 1) * t_new] for h in range(GROUP_HEADS))

    tail = jnp.concatenate([jnp.zeros((ch - t_new, 2 * GROUP_DIM), F32),
                            jnp.concatenate([knew, vnew], axis=1)], axis=0).T
    shifted = pltpu.roll(buf, lb - t_new, axis=1)
    lane = lax.broadcasted_iota(jnp.int32, (2 * GROUP_DIM, ch), 1)
    if lb > ch:
        st_ref[:, 0:lb - ch] = shifted[:, 0:lb - ch]
    st_ref[:, lb - ch:lb] = jnp.where(lane >= ch - t_new, tail, shifted[:, lb - ch:lb])


def _dil_sample(qkv, state_t, prev_out, bb, bn, layer, g):
    db, t, _ = qkv.shape
    nb, _, kv_dim, lb = state_t.shape
    rows = GROUP_HEADS * t
    assert t <= DIL_CHUNK <= lb
    out_spec = pl.BlockSpec((None, t, GROUP_DIM), lambda b: (b, 0, 0))
    st_spec = pl.BlockSpec((None, None, kv_dim, lb), lambda b: (layer, b, 0, 0))
    args = [qkv, state_t, bb, bn]
    in_specs = [
        pl.BlockSpec((None, t, QKV_DIM), lambda b: (b, 0, 0)),
        st_spec,
        pl.BlockSpec((rows, lb), lambda b: (0, 0)),
        pl.BlockSpec((rows, DIL_CHUNK), lambda b: (0, 0)),
    ]
    aliases = {}
    if prev_out is not None:
        args.append(prev_out)
        in_specs.append(pl.BlockSpec(memory_space=pl.ANY))
        aliases = {4: 2}
    return pl.pallas_call(
        functools.partial(_dil_sample_kernel, g=g, t_new=t),
        grid=(db,),
        in_specs=in_specs,
        out_specs=[out_spec, out_spec, st_spec],
        out_shape=[jax.ShapeDtypeStruct((db, t, GROUP_DIM), F32)] * 2 + [jax.ShapeDtypeStruct(state_t.shape, F32)],
        input_output_aliases=aliases,
        compiler_params=_cparams(("parallel",)),
        name=f"dil_sample{g}",
    )(*args)


def _dil_sample_bias(bias_d, g, win, dil, lb, t_new):
    z = BIAS_PAD
    assert DIL_CHUNK <= BIAS_PAD and lb + t_new <= BIAS_LEN
    heads = bias_d[g * GROUP_HEADS:(g + 1) * GROUP_HEADS]
    bb, bn = [], []
    pos = np.arange(lb)
    tk = np.arange(DIL_CHUNK)
    for t in range(t_new):
        dist = lb + t - pos
        vals = heads[:, z + t + 1:z + t + 1 + lb][:, ::-1]
        bb.append(jnp.where(jnp.asarray((dist % dil == 0) & (dist <= win)), vals, NEG))
        vals = heads[:, z + t - (DIL_CHUNK - 1):z + t + 1][:, ::-1]
        ok = (tk < t_new) & (t - tk >= 0) & ((t - tk) % dil == 0) & (t - tk <= win)
        bn.append(jnp.where(jnp.asarray(ok), vals, NEG))
    as_rows = lambda tabs: jnp.stack(tabs, axis=1).reshape(GROUP_HEADS * t_new, -1)
    return as_rows(bb), as_rows(bn)


def _oproj_kernel(h_ref, o_ref, gate_ref, w_ref, out_ref):
    out_ref[...] = h_ref[...] + gate_ref[...] * _dot(o_ref[...].astype(BF16), w_ref[...])


def _oproj_dil_kernel(h_ref, o0_ref, o1_ref, o2_ref, l0_ref, l1_ref, l2_ref, gate_ref, w_ref, out_ref):
    l0, l1, l2 = l0_ref[...], l1_ref[...], l2_ref[...]
    mx = jnp.maximum(jnp.maximum(l0, l1), l2)
    e0, e1, e2 = jnp.exp(l0 - mx), jnp.exp(l1 - mx), jnp.exp(l2 - mx)
    inv = 1.0 / (e0 + e1 + e2)
    o = jnp.concatenate([o0_ref[...] * (e0 * inv), o1_ref[...] * (e1 * inv), o2_ref[...] * (e2 * inv)], axis=1)
    out_ref[...] = h_ref[...] + gate_ref[...] * _dot(o.astype(BF16), w_ref[...])


def _oproj(h, attn, gate, w, tm, tiles_per_seg):
    n, d = h.shape
    row = lambda w: pl.BlockSpec((tm, w), lambda i: (i, 0))
    if isinstance(attn, tuple):
        kern, attn_args = _oproj_dil_kernel, list(attn[0]) + list(attn[1])
        attn_specs = [row(GROUP_DIM)] * 6
    else:
        kern, attn_args, attn_specs = _oproj_kernel, [attn], [row(ATTN_DIM)]
    return pl.pallas_call(
        kern,
        grid=(n // tm,),
        in_specs=[row(d)] + attn_specs + [
            _mod_spec(gate, tm, tiles_per_seg),
            pl.BlockSpec((ATTN_DIM, d), lambda i: (0, 0)),
        ],
        out_specs=row(d),
        out_shape=jax.ShapeDtypeStruct((n, d), F32),
        compiler_params=_cparams(("parallel",)),
        name="oproj",
    )(h, *attn_args, gate, w)


def _topk_rows(jobs, ridx, k):
    nrows = ridx.shape[0]

    def body(it, carry):
        for src_ref, val_ref, idx_ref in jobs:
            a = src_ref[0:nrows, :]
            mx = jnp.max(a, axis=0, keepdims=True)
            ix = jnp.min(jnp.where(a == mx, ridx, 1e9), axis=0, keepdims=True)
            src_ref[0:nrows, :] = jnp.where(ridx == ix, -jnp.inf, a)
            val_ref[pl.ds(it, 1), :] = mx
            idx_ref[pl.ds(it, 1), :] = ix
        return carry

    lax.fori_loop(0, k, body, 0)


def _cand_rows(k):
    return [(a, k // (a + 1)) for a in range(k)]


def _peer_select_kernel(x_ref, g_ref, sh_ref, sc_ref, whi_ref, wlo_ref, sk_ref, a_ref, b_ref, gate_ref,
                        qt_ref, s_ref, s2_ref, t1_ref, i1_ref, t2_ref, i2_ref, tc_ref, ic_ref):
    h = pl.program_id(1)
    k = PEER_TOPK
    half = PEER_DK // 2
    tm = x_ref.shape[0]

    @pl.when(h == 0)
    def _():
        f_hi, f_lo = _split(_norm_mod(x_ref[...], g_ref[...], sh_ref[...], sc_ref[...]))
        whi = whi_ref[...]
        qt_ref[...] = _dot_nt(whi, f_hi) + _dot_nt(wlo_ref[...], f_hi) + _dot_nt(whi, f_lo)

    key_rows = lax.broadcasted_iota(jnp.int32, (PEER_NKEYS, tm), 0).astype(F32)
    row0 = pl.multiple_of(h * PEER_DK, PEER_DK)
    for part, sub_ref in enumerate((s_ref, s2_ref)):
        q_hi, q_lo = _split(qt_ref[pl.ds(row0 + part * half, half), :])
        sub_ref[...] = _dot3(sk_ref[part], q_hi, q_lo)
    _topk_rows([(s_ref, t1_ref, i1_ref), (s2_ref, t2_ref, i2_ref)], key_rows, k)
    off, flat = 0, []
    for a, nb in _cand_rows(k):
        s_ref[off:off + nb, :] = t1_ref[a:a + 1, :] + t2_ref[0:nb, :]
        flat.append(lax.broadcasted_iota(jnp.int32, (nb, tm), 0).astype(F32) + float(a * k))
        off += nb
    n_cand = -(-off // 8) * 8
    s_ref[off:n_cand, :] = jnp.full((n_cand - off, tm), -jnp.inf, F32)
    flat.append(jnp.full((n_cand - off, tm), float(k * k), F32))
    _topk_rows([(s_ref, tc_ref, ic_ref)], jnp.concatenate(flat, axis=0), k)
    jc = ic_ref[...]
    ja = jnp.floor(jc * (1.0 / k))
    jb = jc - ja * k
    e1 = jnp.zeros(jc.shape, F32)
    e2 = jnp.zeros(jc.shape, F32)
    for a in range(k):
        e1 = e1 + jnp.where(ja == float(a), i1_ref[a:a + 1, :], 0.0)
        e2 = e2 + jnp.where(jb == float(a), i2_ref[a:a + 1, :], 0.0)
    top = tc_ref[...]
    e = jnp.exp(top - top[0:1, :])
    a_ref[...] = e1
    b_ref[...] = e2
    gate_ref[...] = e / jnp.sum(e, axis=0, keepdims=True)


def _peer_select(x, g, shift, scale, wqt_hi, wqt_lo, sub_keys, tm, tiles_per_seg):
    n, d = x.shape
    k = PEER_TOPK
    nq = PEER_HEADS * PEER_DK
    out_spec = pl.BlockSpec((k, tm), lambda i, h: (h, i))
    out_shape = jax.ShapeDtypeStruct((PEER_HEADS * k, n), F32)
    mod_spec = lambda mod: pl.BlockSpec((None, mod.shape[1], d), lambda i, h: (i // tiles_per_seg, 0, 0))
    return pl.pallas_call(
        _peer_select_kernel,
        grid=(n // tm, PEER_HEADS),
        in_specs=[
            pl.BlockSpec((tm, d), lambda i, h: (i, 0)),
            pl.BlockSpec((1, d), lambda i, h: (0, 0)),
            mod_spec(shift),
            mod_spec(scale),
            pl.BlockSpec((nq, d), lambda i, h: (0, 0)),
            pl.BlockSpec((nq, d), lambda i, h: (0, 0)),
            pl.BlockSpec((2, PEER_NKEYS, PEER_DK // 2), lambda i, h: (0, 0, 0)),
        ],
        out_specs=[out_spec] * 3,
        out_shape=[out_shape] * 3,
        scratch_shapes=[pltpu.VMEM((nq, tm), F32)] + [pltpu.VMEM((PEER_NKEYS, tm), F32)] * 2
        + [pltpu.VMEM((k, tm), F32)] * 6,
        compiler_params=_cparams(("parallel", "arbitrary")),
        name="peer_select",
    )(x, g.reshape(1, d), shift, scale, wqt_hi, wqt_lo, sub_keys)


def _peer_dense_kernel(x_ref, g_ref, sh_ref, sc_ref, gt_ref, a_ref, b_ref, w_ref, u_ref, v_ref, out_ref,
                       f_ref, gs_ref, at_ref, bt_ref, wt_ref, p_ref, acc_ref):
    e = pl.program_id(1)
    tc = x_ref.shape[0]
    te = u_ref.shape[0]
    nk = PEER_NKEYS
    half = nk // 2
    npair = te // (2 * nk)
    hi_mask = jnp.int32(-65536)

    @pl.when(e == 0)
    def _():
        f_ref[...] = _norm_mod(x_ref[...], g_ref[...], sh_ref[...], sc_ref[...]).astype(BF16)
        acc_ref[...] = jnp.zeros(acc_ref.shape, F32)
        at_ref[...] = a_ref[...].T
        bt_ref[...] = b_ref[...].T
        wt_ref[...] = w_ref[...].T
        key = lax.broadcasted_iota(jnp.int32, (nk, nk), 0).astype(F32)

        def build(c, carry):
            first = jnp.where(key == at_ref[pl.ds(c, 1), :], 1.0, 0.0).astype(BF16)
            second = jnp.where(key == bt_ref[pl.ds(c, 1), :], wt_ref[pl.ds(c, 1), :], 0.0).astype(BF16)
            g = pltpu.bitcast(_dot_nt(first, second), jnp.int32) + 0x8000
            packed = (g[:half] & hi_mask) | lax.shift_right_logical(g[half:], jnp.full((half, nk), 16, jnp.int32))
            for k in range(half // 8):
                gs_ref[pl.ds(pl.multiple_of((k * tc + c) * 8, 8), 8), :] = packed[k * 8:(k + 1) * 8, :]
            return carry

        lax.fori_loop(0, tc, build, 0, unroll=32)

    for r in range(npair):
        p = e * npair + r
        packed = gs_ref[pl.ds((p // 8) * (tc * 8) + p % 8, tc, stride=8), :]
        g_a = pltpu.bitcast(packed & hi_mask, F32)
        g_b = pltpu.bitcast(packed << 16, F32)
        act = _dot_nt(f_ref[...], u_ref[r * 2 * nk:(r + 1) * 2 * nk, :])
        gelu = 0.5 * act * (1.0 + lax.erf(act * (2.0 ** -0.5)))
        p_ref[:, r * 2 * nk:r * 2 * nk + nk] = (g_a * gelu[:, :nk]).astype(BF16)
        p_ref[:, r * 2 * nk + nk:(r + 1) * 2 * nk] = (g_b * gelu[:, nk:]).astype(BF16)
    acc_ref[...] += _dot(p_ref[...], v_ref[...])

    @pl.when(e == pl.num_programs(1) - 1)
    def _():
        out_ref[...] = x_ref[...] + gt_ref[...] * acc_ref[...]


def _pair_permute(tab):
    half = PEER_NKEYS // 2
    t = tab.reshape(tab.shape[:-2] + (2, half, PEER_NKEYS, tab.shape[-1]))
    return jnp.swapaxes(t, -4, -3).reshape(tab.shape).astype(BF16)


def _peer_dense(x, g, shift, scale, gate, first, second, weight, u_tab, v_tab, tc, te, tiles_per_seg):
    n, d = x.shape
    n_exp = u_tab.shape[0]
    nhk = PEER_HEADS * PEER_TOPK
    mod_spec = lambda mod: pl.BlockSpec((None, mod.shape[1], d), lambda i, e: (i // tiles_per_seg, 0, 0))
    sel_spec = pl.BlockSpec((nhk, tc), lambda i, e: (0, i))
    return pl.pallas_call(
        _peer_dense_kernel,
        grid=(n // tc, n_exp // te),
        in_specs=[
            pl.BlockSpec((tc, d), lambda i, e: (i, 0)),
            pl.BlockSpec((1, d), lambda i, e: (0, 0)),
            mod_spec(shift), mod_spec(scale), mod_spec(gate),
            sel_spec, sel_spec, sel_spec,
            pl.BlockSpec((te, d), lambda i, e: (e, 0)),
            pl.BlockSpec((te, d), lambda i, e: (e, 0)),
        ],
        out_specs=pl.BlockSpec((tc, d), lambda i, e: (i, 0)),
        out_shape=jax.ShapeDtypeStruct((n, d), F32),
        scratch_shapes=[
            pltpu.VMEM((tc, d), BF16),
            pltpu.VMEM((tc * PEER_NKEYS // 2, PEER_NKEYS), jnp.int32),
            pltpu.VMEM((tc, nhk), F32),
            pltpu.VMEM((tc, nhk), F32),
            pltpu.VMEM((tc, nhk), F32),
            pltpu.VMEM((tc, te), BF16),
            pltpu.VMEM((tc, d), F32),
        ],
        compiler_params=_cparams(("parallel", "arbitrary")),
        name="peer_dense",
    )(x, g.reshape(1, d), shift, scale, gate, first, second, weight, u_tab, v_tab)


def _final_norm_kernel(x_ref, g_ref, o_ref):
    x = x_ref[...]
    ms = jnp.mean(x * x, axis=-1, keepdims=True)
    o_ref[...] = x * lax.rsqrt(ms + RMS_EPS) * g_ref[...]


def _final_norm(x, g, tm):
    n, d = x.shape
    return pl.pallas_call(
        _final_norm_kernel,
        grid=(n // tm,),
        in_specs=[pl.BlockSpec((tm, d), lambda i: (i, 0)), pl.BlockSpec((1, d), lambda i: (0, 0))],
        out_specs=pl.BlockSpec((tm, d), lambda i: (i, 0)),
        out_shape=jax.ShapeDtypeStruct((n, d), F32),
        compiler_params=_cparams(("parallel",)),
        name="final_norm",
    )(x, g.reshape(1, d))


TM = 256
PEER_TC = 512
PEER_TE = 2048


def _split_w(w):
    hi = w.astype(BF16)
    return hi, (w - hi.astype(F32)).astype(BF16)


def kernel(x_prompt, x_sample, cache_k_moba, cache_v_moba, state_kv_dil0, state_kv_dil1, state_kv_dil2, page_table,
           c_prompt, c_sample, w_ada, b_ada, g_norm, w_qkv, w_o, rel_bias, w_pq, peer_sub_keys, peer_u, peer_v,
           g_final):
    B, S, D = x_prompt.shape
    DB, T, _ = x_sample.shape
    n_a, n_pool = cache_k_moba.shape[:2]
    n_pages = page_table.shape[1]
    past_len = n_pages * PAGE_SIZE
    assert D == D_MODEL and S % MOBA_BLOCK == 0 and past_len % MOBA_BLOCK == 0 and S % TM == 0
    dil_states = (state_kv_dil0, state_kv_dil1, state_kv_dil2)
    for st, (win, dil) in zip(dil_states, DIL_GROUPS):
        assert st.shape[2] == win and win // dil == DIL_CHUNK and S % (dil * DIL_CHUNK) == 0

    np_, ns_ = B * S, DB * T
    tms = min(TM, ns_)
    hp = x_prompt.reshape(np_, D)
    hs = x_sample.reshape(ns_, D)

    mod_all = _ada(jnp.concatenate([c_prompt, c_sample], axis=0), w_ada, b_ada)
    wqkv_bf, wo_bf = w_qkv.astype(BF16), w_o.astype(BF16)
    wpqt_hi, wpqt_lo = _split_w(jnp.swapaxes(w_pq, 1, 2))
    u_bf, v_bf = _pair_permute(peer_u), _pair_permute(peer_v)
    pool_kt = cache_k_moba.transpose(0, 1, 3, 4, 2).reshape(n_a, n_pool, ATTN_DIM, PAGE_SIZE)
    pool_vt = cache_v_moba.transpose(0, 1, 3, 4, 2).reshape(n_a, n_pool, ATTN_DIM, PAGE_SIZE)

    bias_d = _bias_by_distance(rel_bias)
    moba_bias_p = _moba_prompt_bias(bias_d, S)
    moba_bias_s = _moba_sample_bias(bias_d, past_len, T)
    dil_bias_p = [_dil_prompt_bias(bias_d, g, w, d) for g, (w, d) in enumerate(DIL_GROUPS)]
    dil_bias_s = [_dil_sample_bias(bias_d, g, w, d, dil_states[g].shape[2], T)
                  for g, (w, d) in enumerate(DIL_GROUPS)]

    states_t = [st.transpose(0, 1, 3, 4, 5, 2).reshape(st.shape[0], DB, 2 * GROUP_DIM, st.shape[2])
                for st in dil_states]

    kp_rows, vp_rows, ks_rows, vs_rows = [], [], [], []
    dil_p = [[] for _ in DIL_GROUPS]
    dil_s = [None for _ in DIL_GROUPS]
    for i in range(DEPTH):
        j = i // 2
        mod = mod_all[i].reshape(B + DB, 6, D)
        mp = [mod[:B, c].reshape(B, 1, D) for c in range(6)]
        ms = [jnp.repeat(mod[B:, c], T, axis=0).reshape(ns_ // tms, tms, D) for c in range(6)]
        seg_p, seg_s = S // TM, 1

        qkv_p = _nmm(hp, g_norm[i, 0], mp[0], mp[1], wqkv_bf[i], TM, seg_p)
        qkv_s = _nmm(hs, g_norm[i, 0], ms[0], ms[1], wqkv_bf[i], tms, seg_s)
        qkv_p3 = qkv_p.reshape(B, S, QKV_DIM)
        qkv_s3 = qkv_s.reshape(DB, T, QKV_DIM)
        heads = lambda a, n, l: a.reshape(n, l, 3, N_HEADS, HEAD_DIM)
        qkv_p5, qkv_s5 = heads(qkv_p, B, S), heads(qkv_s, DB, T)
        if i % 2 == 0:
            attn_p = _moba_prompt(qkv_p3, moba_bias_p).reshape(np_, ATTN_DIM)
            attn_s = _moba_sample(qkv_s3, pool_kt, pool_vt, page_table, moba_bias_s, j, n_pages).reshape(ns_, ATTN_DIM)
            kp_rows.append(qkv_p5[:, :, 1])
            vp_rows.append(qkv_p5[:, :, 2])
            ks_rows.append(qkv_s5[:, :, 1])
            vs_rows.append(qkv_s5[:, :, 2])
        else:
            op, lp, os_, ls = [], [], [], []
            for g, (win, dil) in enumerate(DIL_GROUPS):
                o, lse = _dil_prompt(qkv_p3, dil_bias_p[g], g, dil)
                op.append(o.reshape(np_, GROUP_DIM))
                lp.append(lse.reshape(np_, GROUP_DIM))
                o, lse, dil_s[g] = _dil_sample(qkv_s3, states_t[g], dil_s[g], *dil_bias_s[g], j, g)
                os_.append(o.reshape(ns_, GROUP_DIM))
                ls.append(lse.reshape(ns_, GROUP_DIM))
                hsl = slice(g * GROUP_HEADS, (g + 1) * GROUP_HEADS)
                kv_p = jnp.stack([qkv_p5[:, :, 1, hsl], qkv_p5[:, :, 2, hsl]], axis=2)
                dil_p[g].append(kv_p[:, S - min(win, S):])
            attn_p, attn_s = (op, lp), (os_, ls)
        hp = _oproj(hp, attn_p, mp[2], wo_bf[i], TM, seg_p)
        hs = _oproj(hs, attn_s, ms[2], wo_bf[i], tms, seg_s)

        sel_p = _peer_select(hp, g_norm[i, 1], mp[3], mp[4], wpqt_hi[i], wpqt_lo[i], peer_sub_keys[i], TM, seg_p)
        sel_s = _peer_select(hs, g_norm[i, 1], ms[3], ms[4], wpqt_hi[i], wpqt_lo[i], peer_sub_keys[i], tms, seg_s)
        hp = _peer_dense(hp, g_norm[i, 1], mp[3], mp[4], mp[5], *sel_p, u_bf[i], v_bf[i], PEER_TC, PEER_TE,
                         S // PEER_TC)
        hs = _peer_dense(hs, g_norm[i, 1], ms[3], ms[4], ms[5], *sel_s, u_bf[i], v_bf[i], tms, PEER_TE, 1)

    y_prompt = _final_norm(hp, g_final, TM).reshape(B, S, D)
    y_sample = _final_norm(hs, g_final, tms).reshape(DB, T, D)
    return (y_prompt, y_sample,
            jnp.stack(kp_rows), jnp.stack(vp_rows), jnp.stack(ks_rows), jnp.stack(vs_rows),
            jnp.stack(dil_p[0]), jnp.stack(dil_p[1]), jnp.stack(dil_p[2]),
            *[st.reshape(st.shape[:2] + (2, GROUP_HEADS, HEAD_DIM, st.shape[3])).transpose(0, 1, 5, 2, 3, 4)
              for st in dil_s])
```

```python
import functools
import math

import numpy as np
import jax
import jax.numpy as jnp
from jax import lax
from jax.experimental import pallas as pl
from jax.experimental.pallas import tpu as pltpu

F32 = jnp.float32
BF16 = jnp.bfloat16
NEG = -1e30

D_MODEL = 1024
N_HEADS = 12
HEAD_DIM = 64
ATTN_DIM = N_HEADS * HEAD_DIM
QKV_DIM = 3 * ATTN_DIM
ATTN_SCALE = HEAD_DIM ** -0.5
DEPTH = 4
PAGE_SIZE = 128
MOBA_BLOCK = 256
MOBA_TOPK = 3
DIL_GROUPS = ((128, 1), (512, 4), (2048, 16))
GROUP_HEADS = 4
GROUP_DIM = GROUP_HEADS * HEAD_DIM
DIL_CHUNK = 128
REL_BUCKETS = 32
REL_MAX_DIST = 2048
PEER_HEADS = 8
PEER_NKEYS = 128
PEER_DK = 256
PEER_TOPK = 16
RMS_EPS = 1e-6

VMEM_LIMIT = 56 * 1024 * 1024


def _cparams(sem):
    return pltpu.CompilerParams(dimension_semantics=sem, vmem_limit_bytes=VMEM_LIMIT)


def _split(x):
    hi = x.astype(BF16)
    lo = (x - hi.astype(F32)).astype(BF16)
    return hi, lo


def _dot(a, b):
    return jnp.dot(a, b, preferred_element_type=F32)


def _dot_nt(a, b):
    return lax.dot_general(a, b, (((1,), (1,)), ((), ())), preferred_element_type=F32)


def _dot3(a, b_hi, b_lo):
    a_hi, a_lo = _split(a)
    return _dot(a_hi, b_hi) + _dot(a_lo, b_hi) + _dot(a_hi, b_lo)


def _dot3_nt(a, b):
    a_hi, a_lo = _split(a)
    b_hi, b_lo = _split(b)
    return _dot_nt(a_hi, b_hi) + _dot_nt(a_lo, b_hi) + _dot_nt(a_hi, b_lo)


def _norm_mod(x, g, shift, scale):
    ms = jnp.mean(x * x, axis=-1, keepdims=True)
    xn = x * lax.rsqrt(ms + RMS_EPS) * g
    return xn * (1.0 + scale) + shift


def _bucket_np(dist):
    n = np.maximum(dist, 0)
    max_exact = REL_BUCKETS // 2
    nf = np.maximum(n, 1).astype(np.float64)
    large = max_exact + (np.log(nf / max_exact) / math.log(REL_MAX_DIST / max_exact)
                         * (REL_BUCKETS - max_exact)).astype(np.int64)
    large = np.minimum(large, REL_BUCKETS - 1)
    return np.where(n < max_exact, n, large).astype(np.int32)


BIAS_PAD = 256
BIAS_LEN = 4352


def _bias_by_distance(rel_bias):
    onehot = np.zeros((BIAS_LEN, REL_BUCKETS), np.float32)
    onehot[np.arange(BIAS_LEN), _bucket_np(np.arange(BIAS_LEN))] = 1.0
    by_dist = jnp.dot(jnp.asarray(onehot), rel_bias, precision=lax.Precision.HIGHEST)
    return jnp.concatenate([jnp.full((N_HEADS, BIAS_PAD), NEG, F32), by_dist.T], axis=1)


def _toeplitz(wp, rows, cols):
    n2 = wp.shape[-1]
    x = jnp.tile(wp, (1,) * (wp.ndim - 1) + (rows,))[..., :rows * (n2 - 1)]
    return x.reshape(wp.shape[:-1] + (rows, n2 - 1))[..., :cols]


def _rank_select(s, lane_n, n_valid, topk):
    nb = s.shape[1]
    rank = jnp.zeros(s.shape, F32)
    for m in range(nb):
        sm = s[:, m:m + 1]
        beats = (sm > s) | ((sm == s) & (m < lane_n))
        beats = beats & (m < n_valid)
        rank = rank + jnp.where(beats, 1.0, 0.0)
    sel = (lane_n < n_valid) & (rank < float(topk))
    return jnp.where(sel, 1.0, 0.0)


def _rank_select_rows(s, row_n, n_valid, topk):
    nb = s.shape[0]
    rank = jnp.zeros(s.shape, F32)
    for m in range(nb):
        sm = s[m:m + 1, :]
        beats = (sm > s) | ((sm == s) & (m < row_n))
        beats = beats & (m < n_valid)
        rank = rank + jnp.where(beats, 1.0, 0.0)
    sel = (row_n < n_valid) & (rank < float(topk))
    return jnp.where(sel, 1.0, 0.0)


def _ada_kernel(c_ref, w_ref, b_ref, o_ref):
    c = c_ref[...]
    s = c * (1.0 / (1.0 + jnp.exp(-c)))
    w_hi, w_lo = _split(w_ref[...])
    o_ref[...] = _dot3(s, w_hi, w_lo) + b_ref[...]


def _ada(c_all, w_ada, b_ada):
    n = c_all.shape[0]
    depth, d, n6 = w_ada.shape
    tn = 1536
    return pl.pallas_call(
        _ada_kernel,
        grid=(depth, n6 // tn),
        in_specs=[
            pl.BlockSpec((n, d), lambda i, j: (0, 0)),
            pl.BlockSpec((None, d, tn), lambda i, j: (i, 0, j)),
            pl.BlockSpec((None, 1, tn), lambda i, j: (i, 0, j)),
        ],
        out_specs=pl.BlockSpec((None, n, tn), lambda i, j: (i, 0, j)),
        out_shape=jax.ShapeDtypeStruct((depth, n, n6), F32),
        compiler_params=_cparams(("parallel", "parallel")),
        name="ada",
    )(c_all, w_ada, b_ada.reshape(depth, 1, n6))


def _nmm_kernel(x_ref, g_ref, sh_ref, sc_ref, w_ref, o_ref, *kvt_refs):
    f = _norm_mod(x_ref[...], g_ref[...], sh_ref[...], sc_ref[...])
    out = _dot(f.astype(BF16), w_ref[...])
    o_ref[...] = out
    for part, t_ref in enumerate(kvt_refs, start=1):
        t_ref[...] = out[:, part * ATTN_DIM:(part + 1) * ATTN_DIM].T


def _mod_spec(mod, tm, tiles_per_seg):
    r = mod.shape[1]
    return pl.BlockSpec((None, r, D_MODEL), lambda i, *_: (i // tiles_per_seg, 0, 0))


def _nmm(x, g, shift, scale, w, tm, tiles_per_seg, kv_transposed=False):
    n, d = x.shape
    nout = w.shape[1]
    out_specs = [pl.BlockSpec((tm, nout), lambda i: (i, 0))]
    out_shape = [jax.ShapeDtypeStruct((n, nout), F32)]
    if kv_transposed:
        n_seq = n // (tm * tiles_per_seg)
        t_spec = pl.BlockSpec((None, ATTN_DIM, tm), lambda i: (i // tiles_per_seg, 0, i % tiles_per_seg))
        out_specs += [t_spec, t_spec]
        out_shape += [jax.ShapeDtypeStruct((n_seq, ATTN_DIM, tm * tiles_per_seg), F32)] * 2
    res = pl.pallas_call(
        _nmm_kernel,
        grid=(n // tm,),
        in_specs=[
            pl.BlockSpec((tm, d), lambda i: (i, 0)),
            pl.BlockSpec((1, d), lambda i: (0, 0)),
            _mod_spec(shift, tm, tiles_per_seg),
            _mod_spec(scale, tm, tiles_per_seg),
            pl.BlockSpec((d, nout), lambda i: (0, 0)),
        ],
        out_specs=out_specs,
        out_shape=out_shape,
        compiler_params=_cparams(("parallel",)),
        name="nmm",
    )(x, g.reshape(1, d), shift, scale, w)
    return res if kv_transposed else res[0]


def _moba_prompt_kernel(q_ref, k_ref, v_ref, bias_ref, o_ref, kmean_ref, mask_ref):
    i = pl.program_id(2)
    blk = MOBA_BLOCK
    nb = k_ref.shape[0] // blk

    @pl.when(i == 0)
    def _():
        for n in range(nb):
            kmean_ref[n:n + 1, :] = jnp.mean(k_ref[n * blk:(n + 1) * blk, :], axis=0, keepdims=True)

    lane_head = lax.broadcasted_iota(jnp.int32, (blk, 128), 1) // HEAD_DIM
    row_n = lax.broadcasted_iota(jnp.int32, (nb, blk), 0)
    km_head = lax.broadcasted_iota(jnp.int32, (nb, 128), 1) // HEAD_DIM
    q2 = q_ref[...] * ATTN_SCALE
    qb = []
    for hh in range(2):
        q = jnp.where(lane_head == hh, q2, 0.0)
        scores = _dot3_nt(jnp.where(km_head == hh, kmean_ref[...], 0.0), q)
        sel = _rank_select_rows(scores, row_n, i, MOBA_TOPK)
        sel = jnp.concatenate([sel, jnp.zeros((128 - nb, blk), F32)], axis=0).T
        for n in range(nb):
            mask_ref[hh, n] = jnp.broadcast_to(sel[:, n:n + 1], (blk, 128))
        qb.append(q.astype(BF16))

    def tile(j, hh, masked):
        r0 = pl.multiple_of(j * blk, blk)
        logits = _dot_nt(qb[hh], k_ref[pl.ds(r0, blk), :].astype(BF16)) + bias_ref[i - j, hh]
        if masked:
            keep = mask_ref[hh, j] > 0.5
            logits = jnp.where(jnp.concatenate([keep] * (blk // 128), axis=1), logits, NEG)
        return logits, v_ref[pl.ds(r0, blk), :].astype(BF16)

    state = []
    for hh in range(2):
        logits, vb = tile(i, hh, False)
        m = jnp.max(logits, axis=1, keepdims=True)
        p = jnp.exp(logits - m)
        state += [m, jnp.sum(p, axis=1, keepdims=True), _dot(p.astype(BF16), vb)]

    def body(u, carry):
        out = []
        for hh in range(2):
            m, l, acc = carry[3 * hh:3 * hh + 3]
            lg0, v0 = tile(2 * u, hh, True)
            lg1, v1 = tile(2 * u + 1, hh, True)
            m_new = jnp.maximum(m, jnp.maximum(jnp.max(lg0, axis=1, keepdims=True), jnp.max(lg1, axis=1, keepdims=True)))
            alpha = jnp.exp(m - m_new)
            p0 = jnp.exp(lg0 - m_new)
            p1 = jnp.exp(lg1 - m_new)
            l = alpha * l + jnp.sum(p0, axis=1, keepdims=True) + jnp.sum(p1, axis=1, keepdims=True)
            acc = alpha * acc + _dot(p0.astype(BF16), v0) + _dot(p1.astype(BF16), v1)
            out += [m_new, l, acc]
        return tuple(out)

    _, l0, acc0, _, l1, acc1 = lax.fori_loop(0, (i + 1) // 2, body, tuple(state))
    o_ref[...] = jnp.where(lane_head == 0, acc0 / l0, acc1 / l1)


def _moba_prompt(qkv, bias):
    b, s, _ = qkv.shape
    blk = MOBA_BLOCK
    nb = s // blk
    npair = N_HEADS // 2
    return pl.pallas_call(
        _moba_prompt_kernel,
        grid=(npair, b, nb),
        in_specs=[
            pl.BlockSpec((None, blk, 128), lambda hp, bb, i: (bb, i, hp)),
            pl.BlockSpec((None, s, 128), lambda hp, bb, i: (bb, 0, npair + hp)),
            pl.BlockSpec((None, s, 128), lambda hp, bb, i: (bb, 0, 2 * npair + hp)),
            pl.BlockSpec((nb, 2, blk, blk), lambda hp, bb, i: (0, hp, 0, 0)),
        ],
        out_specs=pl.BlockSpec((None, blk, 128), lambda hp, bb, i: (bb, i, hp)),
        out_shape=jax.ShapeDtypeStruct((b, s, ATTN_DIM), F32),
        scratch_shapes=[pltpu.VMEM((nb, 128), F32), pltpu.VMEM((2, nb, blk, 128), F32)],
        compiler_params=_cparams(("parallel", "parallel", "arbitrary")),
        name="moba_prompt",
    )(qkv, qkv, qkv, bias)


def _moba_prompt_bias(bias_d, s):
    blk = MOBA_BLOCK
    nb = s // blk
    assert blk <= BIAS_PAD
    wps = []
    for delta in range(nb):
        z = BIAS_PAD + delta * blk
        lo = bias_d[:, z - blk:z + 1][:, ::-1]
        hi = bias_d[:, z + 1:z + blk][:, ::-1]
        wps.append(jnp.concatenate([lo, hi], axis=1))
    return _toeplitz(jnp.stack(wps), blk, blk)


def _moba_sample_kernel(pt_ref, new_ref, *refs, n_pages, t_new):
    del pt_ref
    kt_refs, vt_refs = refs[:n_pages], refs[n_pages:2 * n_pages]
    sbias_ref, o_ref, lg_ref, p_ref = refs[2 * n_pages:]
    rows = N_HEADS * t_new
    pages_per_block = MOBA_BLOCK // PAGE_SIZE
    n_blocks = n_pages // pages_per_block
    r_head = lax.broadcasted_iota(jnp.int32, (rows, ATTN_DIM), 0) // t_new
    c_head = lax.broadcasted_iota(jnp.int32, (rows, ATTN_DIM), 1) // HEAD_DIM
    pad = jnp.zeros((PAGE_SIZE - t_new, ATTN_DIM), F32)

    q = new_ref[:, 0:ATTN_DIM] * ATTN_SCALE
    qhi, qlo = _split(jnp.where(r_head == c_head, jnp.concatenate([q] * N_HEADS, axis=0), 0.0))

    scores = []
    for n in range(n_blocks):
        ksum = None
        for pg in range(n * pages_per_block, (n + 1) * pages_per_block):
            kt = kt_refs[pg][...]
            lg_ref[pg] = _dot(qhi, kt.astype(BF16))
            ksum = kt if ksum is None else ksum + kt
        k_hi, k_lo = _split(ksum)
        y = _dot(qhi, k_hi) + _dot(qlo, k_hi) + _dot(qhi, k_lo)
        scores.append(jnp.sum(y, axis=1, keepdims=True))
    s = jnp.concatenate(scores, axis=1)
    lane_n = lax.broadcasted_iota(jnp.int32, (rows, n_blocks), 1)
    sel = _rank_select(s, lane_n, n_blocks, MOBA_TOPK)
    knew = jnp.concatenate([new_ref[:, ATTN_DIM:2 * ATTN_DIM], pad], axis=0)
    lg_ref[n_pages] = _dot_nt(qhi, knew.astype(BF16))

    mrun = jnp.full((rows, PAGE_SIZE), NEG, F32)
    for pg in range(n_pages + 1):
        lgt = lg_ref[pg] + sbias_ref[pg]
        if pg < n_pages:
            n = pg // pages_per_block
            lgt = jnp.where(sel[:, n:n + 1] > 0.5, lgt, NEG)
        lg_ref[pg] = lgt
        mrun = jnp.maximum(mrun, lgt)
    m = jnp.max(mrun, axis=1, keepdims=True)
    srun = jnp.zeros((rows, PAGE_SIZE), F32)
    for pg in range(n_pages + 1):
        e = jnp.exp(lg_ref[pg] - m)
        srun = srun + e
        p_ref[pg] = e.astype(BF16)
    linv = 1.0 / jnp.sum(srun, axis=1, keepdims=True)

    vnew = jnp.concatenate([new_ref[:, 2 * ATTN_DIM:3 * ATTN_DIM], pad], axis=0)
    acc = _dot(p_ref[n_pages], vnew.astype(BF16))
    for pg in range(n_pages):
        acc = acc + _dot_nt(p_ref[pg], vt_refs[pg][...].astype(BF16))
    acc = jnp.where(r_head == c_head, acc * linv, 0.0)
    out = acc[0:t_new, :]
    for h in range(1, N_HEADS):
        out = out + acc[h * t_new:(h + 1) * t_new, :]
    o_ref[...] = out


def _moba_sample(qkv, pool_kt, pool_vt, page_table, sbias, layer, n_pages):
    db, t, _ = qkv.shape
    rows = N_HEADS * t
    page_spec = lambda pg: pl.BlockSpec((None, None, ATTN_DIM, PAGE_SIZE), lambda b, pt: (layer, pt[b, pg], 0, 0))
    grid_spec = pltpu.PrefetchScalarGridSpec(
        num_scalar_prefetch=1,
        grid=(db,),
        in_specs=[pl.BlockSpec((None, t, QKV_DIM), lambda b, pt: (b, 0, 0))]
        + [page_spec(pg) for pg in range(n_pages)] * 2
        + [pl.BlockSpec((n_pages + 1, rows, PAGE_SIZE), lambda b, pt: (0, 0, 0))],
        out_specs=pl.BlockSpec((None, t, ATTN_DIM), lambda b, pt: (b, 0, 0)),
        scratch_shapes=[
            pltpu.VMEM((n_pages + 1, rows, PAGE_SIZE), F32),
            pltpu.VMEM((n_pages + 1, rows, PAGE_SIZE), BF16),
        ],
    )
    return pl.pallas_call(
        functools.partial(_moba_sample_kernel, n_pages=n_pages, t_new=t),
        grid_spec=grid_spec,
        out_shape=jax.ShapeDtypeStruct((db, t, ATTN_DIM), F32),
        compiler_params=_cparams(("parallel",)),
        name="moba_sample",
    )(page_table, qkv, *([pool_kt] * n_pages), *([pool_vt] * n_pages), sbias)


def _moba_sample_bias(bias_d, past_len, t):
    n_pages = past_len // PAGE_SIZE
    z = BIAS_PAD
    assert PAGE_SIZE <= BIAS_PAD and past_len + t <= BIAS_LEN
    past = jnp.stack([bias_d[:, z + tq + 1:z + tq + 1 + past_len][:, ::-1] for tq in range(t)], axis=1)
    past = past.reshape(N_HEADS * t, n_pages, PAGE_SIZE).transpose(1, 0, 2)
    new = jnp.stack([bias_d[:, z + tq - (PAGE_SIZE - 1):z + tq + 1][:, ::-1] for tq in range(t)], axis=1)
    return jnp.concatenate([past, new.reshape(1, N_HEADS * t, PAGE_SIZE)], axis=0)


def _dil_prompt_kernel(*refs, dil):
    q_refs, k_refs, v_refs = refs[0:2], refs[2:4], refs[4:6]
    bias_ref = refs[6]
    o_refs, lse_refs = refs[7:9], refs[9:11]
    r, c = pl.program_id(1), pl.program_id(2)
    ch = DIL_CHUNK
    rows = lambda chunk: pl.ds(r + dil * ch * chunk, ch, stride=dil) if dil > 1 else pl.ds(ch * chunk, ch)
    cur, prev = rows(c), rows(jnp.maximum(c - 1, 0))
    col = lax.broadcasted_iota(jnp.int32, (ch, 2 * ch), 1)
    keep = (col >= ch) | (c > 0)
    lane_head = lax.broadcasted_iota(jnp.int32, (ch, 2 * HEAD_DIM), 1) // HEAD_DIM
    for pc in range(2):
        q = q_refs[pc][cur, :] * ATTN_SCALE
        k = jnp.concatenate([k_refs[pc][prev, :], k_refs[pc][cur, :]], axis=0).astype(BF16)
        v = jnp.concatenate([v_refs[pc][prev, :], v_refs[pc][cur, :]], axis=0).astype(BF16)
        o_h, lse_h = [], []
        for hh in range(2):
            qh = jnp.where(lane_head == hh, q, 0.0).astype(BF16)
            s = jnp.where(keep, _dot_nt(qh, k) + bias_ref[2 * pc + hh], NEG)
            m = jnp.max(s, axis=1, keepdims=True)
            p = jnp.exp(s - m)
            l = jnp.sum(p, axis=1, keepdims=True)
            o_h.append(_dot(p.astype(BF16), v) / l)
            lse_h.append(m + jnp.log(l))
        o_refs[pc][cur, :] = jnp.where(lane_head == 0, o_h[0], o_h[1])
        lse_refs[pc][cur, :] = jnp.where(lane_head == 0, lse_h[0], lse_h[1])


def _dil_prompt(qkv, bias, g, dil):
    b, s, _ = qkv.shape
    nch = s // dil // DIL_CHUNK
    pieces = ATTN_DIM // 128
    col = lambda part, pc: pl.BlockSpec((None, s, 128), lambda bb, r, c: (bb, 0, part * pieces + 2 * g + pc))
    out_spec = pl.BlockSpec((None, s, 128), lambda bb, r, c: (bb, 0, 0))
    res = pl.pallas_call(
        functools.partial(_dil_prompt_kernel, dil=dil),
        grid=(b, dil, nch),
        in_specs=[col(part, pc) for part in range(3) for pc in range(2)]
        + [pl.BlockSpec((GROUP_HEADS, DIL_CHUNK, 2 * DIL_CHUNK), lambda bb, r, c: (0, 0, 0))],
        out_specs=[out_spec] * 4,
        out_shape=[jax.ShapeDtypeStruct((b, s, 128), F32)] * 4,
        compiler_params=_cparams(("parallel", "arbitrary", "arbitrary")),
        name=f"dil_prompt{g}",
    )(*([qkv] * 6), bias)
    return res[0:2], res[2:4]


def _dil_prompt_bias(bias_d, g, win, dil):
    ch = DIL_CHUNK
    assert win // dil == ch
    z = BIAS_PAD
    heads = bias_d[g * GROUP_HEADS:(g + 1) * GROUP_HEADS]
    steps = heads[:, z:z + win + 1:dil][:, ::-1]
    wp = jnp.concatenate([steps, jnp.full((GROUP_HEADS, 4 * ch - (ch + 1)), NEG, F32)], axis=1)
    return _toeplitz(wp, ch, 2 * ch)


def _dil_sample_kernel(new_ref, buf_ref, bb_ref, bn_ref, *rest, g, t_new):
    o_ref, lse_ref, st_ref = rest[-3:]
    lb = buf_ref.shape[1]
    rows = GROUP_HEADS * t_new
    ch = DIL_CHUNK
    r_head = lax.broadcasted_iota(jnp.int32, (rows, GROUP_DIM), 0) // t_new
    c_head = lax.broadcasted_iota(jnp.int32, (rows, GROUP_DIM), 1) // HEAD_DIM
    diag = r_head == c_head
    pad = jnp.zeros((ch - t_new, GROUP_DIM), F32)
    qcol = g * GROUP_DIM
    kcol = ATTN_DIM + g * GROUP_DIM
    vcol = 2 * ATTN_DIM + g * GROUP_DIM
    knew, vnew = new_ref[:, kcol:kcol + GROUP_DIM], new_ref[:, vcol:vcol + GROUP_DIM]
    q = new_ref[:, qcol:qcol + GROUP_DIM] * ATTN_SCALE
    qbd = jnp.where(diag, jnp.concatenate([q] * GROUP_HEADS, axis=0), 0.0).astype(BF16)
    buf = buf_ref[...]
    kt, vt = buf[:GROUP_DIM].astype(BF16), buf[GROUP_DIM:].astype(BF16)
    lgb = _dot(qbd, kt) + bb_ref[...]
    lgn = _dot_nt(qbd, jnp.concatenate([knew, pad], axis=0).astype(BF16)) + bn_ref[...]
    m = jnp.maximum(jnp.max(lgb, axis=1, keepdims=True), jnp.max(lgn, axis=1, keepdims=True))
    pb = jnp.exp(lgb - m)
    pn = jnp.exp(lgn - m)
    l = jnp.sum(pb, axis=1, keepdims=True) + jnp.sum(pn, axis=1, keepdims=True)
    o = _dot_nt(pb.astype(BF16), vt) + _dot(pn.astype(BF16), jnp.concatenate([vnew, pad], axis=0).astype(BF16))
    o = jnp.where(diag, o / l, 0.0)
    lse = jnp.where(diag, m + jnp.log(l), 0.0)
    o_ref[...] = sum(o[h * t_new:(h + 1) * t_new] for h in range(GROUP_HEADS))
    lse_ref[...] = sum(lse[h * t_new:(h + 1) * t_new] for h in range(GROUP_HEADS))

    tail = jnp.concatenate([jnp.zeros((ch - t_new, 2 * GROUP_DIM), F32),
                            jnp.concatenate([knew, vnew], axis=1)], axis=0).T
    shifted = pltpu.roll(buf, lb - t_new, axis=1)
    lane = lax.broadcasted_iota(jnp.int32, (2 * GROUP_DIM, ch), 1)
    if lb > ch:
        st_ref[:, 0:lb - ch] = shifted[:, 0:lb - ch]
    st_ref[:, lb - ch:lb] = jnp.where(lane >= ch - t_new, tail, shifted[:, lb - ch:lb])


def _dil_sample(qkv, state_t, prev_out, bb, bn, layer, g):
    db, t, _ = qkv.shape
    nb, _, kv_dim, lb = state_t.shape
    rows = GROUP_HEADS * t
    assert t <= DIL_CHUNK <= lb
    out_spec = pl.BlockSpec((None, t, GROUP_DIM), lambda b: (b, 0, 0))
    st_spec = pl.BlockSpec((None, None, kv_dim, lb), lambda b: (layer, b, 0, 0))
    args = [qkv, state_t, bb, bn]
    in_specs = [
        pl.BlockSpec((None, t, QKV_DIM), lambda b: (b, 0, 0)),
        st_spec,
        pl.BlockSpec((rows, lb), lambda b: (0, 0)),
        pl.BlockSpec((rows, DIL_CHUNK), lambda b: (0, 0)),
    ]
    aliases = {}
    if prev_out is not None:
        args.append(prev_out)
        in_specs.append(pl.BlockSpec(memory_space=pl.ANY))
        aliases = {4: 2}
    return pl.pallas_call(
        functools.partial(_dil_sample_kernel, g=g, t_new=t),
        grid=(db,),
        in_specs=in_specs,
        out_specs=[out_spec, out_spec, st_spec],
        out_shape=[jax.ShapeDtypeStruct((db, t, GROUP_DIM), F32)] * 2 + [jax.ShapeDtypeStruct(state_t.shape, F32)],
        input_output_aliases=aliases,
        compiler_params=_cparams(("parallel",)),
        name=f"dil_sample{g}",
    )(*args)


def _dil_sample_bias(bias_d, g, win, dil, lb, t_new):
    z = BIAS_PAD
    assert DIL_CHUNK <= BIAS_PAD and lb + t_new <= BIAS_LEN
    heads = bias_d[g * GROUP_HEADS:(g + 1) * GROUP_HEADS]
    bb, bn = [], []
    pos = np.arange(lb)
    tk = np.arange(DIL_CHUNK)
    for t in range(t_new):
        dist = lb + t - pos
        vals = heads[:, z + t + 1:z + t + 1 + lb][:, ::-1]
        bb.append(jnp.where(jnp.asarray((dist % dil == 0) & (dist <= win)), vals, NEG))
        vals = heads[:, z + t - (DIL_CHUNK - 1):z + t + 1][:, ::-1]
        ok = (tk < t_new) & (t - tk >= 0) & ((t - tk) % dil == 0) & (t - tk <= win)
        bn.append(jnp.where(jnp.asarray(ok), vals, NEG))
    as_rows = lambda tabs: jnp.stack(tabs, axis=1).reshape(GROUP_HEADS * t_new, -1)
    return as_rows(bb), as_rows(bn)


def _oproj_kernel(h_ref, o_ref, gate_ref, w_ref, out_ref):
    out_ref[...] = h_ref[...] + gate_ref[...] * _dot(o_ref[...].astype(BF16), w_ref[...])


def _oproj_dil_kernel(h_ref, *refs):
    n_groups = len(DIL_GROUPS)
    gate_ref, w_ref, out_ref = refs[-3:]
    npc = (len(refs) - 3) // (2 * n_groups)
    o_refs, l_refs = refs[:n_groups * npc], refs[n_groups * npc:2 * n_groups * npc]
    scaled = [None] * (n_groups * npc)
    for pc in range(npc):
        lses = [l_refs[g * npc + pc][...] for g in range(n_groups)]
        mx = functools.reduce(jnp.maximum, lses)
        es = [jnp.exp(l - mx) for l in lses]
        inv = 1.0 / sum(es[1:], es[0])
        for g in range(n_groups):
            scaled[g * npc + pc] = o_refs[g * npc + pc][...] * (es[g] * inv)
    o = jnp.concatenate(scaled, axis=1)
    out_ref[...] = h_ref[...] + gate_ref[...] * _dot(o.astype(BF16), w_ref[...])


def _oproj(h, attn, gate, w, tm, tiles_per_seg):
    n, d = h.shape
    row = lambda w: pl.BlockSpec((tm, w), lambda i: (i, 0))
    if isinstance(attn, tuple):
        kern, attn_args = _oproj_dil_kernel, list(attn[0]) + list(attn[1])
        attn_specs = [row(a.shape[1]) for a in attn_args]
    else:
        kern, attn_args, attn_specs = _oproj_kernel, [attn], [row(ATTN_DIM)]
    return pl.pallas_call(
        kern,
        grid=(n // tm,),
        in_specs=[row(d)] + attn_specs + [
            _mod_spec(gate, tm, tiles_per_seg),
            pl.BlockSpec((ATTN_DIM, d), lambda i: (0, 0)),
        ],
        out_specs=row(d),
        out_shape=jax.ShapeDtypeStruct((n, d), F32),
        compiler_params=_cparams(("parallel",)),
        name="oproj",
    )(h, *attn_args, gate, w)


def _topk_rows(jobs, ridx, k):
    nrows = ridx.shape[0]

    def body(it, carry):
        for src_ref, val_ref, idx_ref in jobs:
            a = src_ref[0:nrows, :]
            mx = jnp.max(a, axis=0, keepdims=True)
            ix = jnp.min(jnp.where(a == mx, ridx, 1e9), axis=0, keepdims=True)
            src_ref[0:nrows, :] = jnp.where(ridx == ix, -jnp.inf, a)
            val_ref[pl.ds(it, 1), :] = mx
            idx_ref[pl.ds(it, 1), :] = ix
        return carry

    lax.fori_loop(0, k, body, 0)


def _cand_rows(k):
    return [(a, k // (a + 1)) for a in range(k)]


def _peer_select_kernel(x_ref, g_ref, sh_ref, sc_ref, whi_ref, wlo_ref, sk_ref, a_ref, b_ref, gate_ref,
                        qt_ref, s_ref, s2_ref, t1_ref, i1_ref, t2_ref, i2_ref, tc_ref, ic_ref):
    h = pl.program_id(1)
    k = PEER_TOPK
    half = PEER_DK // 2
    tm = x_ref.shape[0]

    @pl.when(h == 0)
    def _():
        f_hi, f_lo = _split(_norm_mod(x_ref[...], g_ref[...], sh_ref[...], sc_ref[...]))
        whi = whi_ref[...]
        qt_ref[...] = _dot_nt(whi, f_hi) + _dot_nt(wlo_ref[...], f_hi) + _dot_nt(whi, f_lo)

    key_rows = lax.broadcasted_iota(jnp.int32, (PEER_NKEYS, tm), 0).astype(F32)
    row0 = pl.multiple_of(h * PEER_DK, PEER_DK)
    for part, sub_ref in enumerate((s_ref, s2_ref)):
        q_hi, q_lo = _split(qt_ref[pl.ds(row0 + part * half, half), :])
        sub_ref[...] = _dot3(sk_ref[part], q_hi, q_lo)
    _topk_rows([(s_ref, t1_ref, i1_ref), (s2_ref, t2_ref, i2_ref)], key_rows, k)
    off, flat = 0, []
    for a, nb in _cand_rows(k):
        s_ref[off:off + nb, :] = t1_ref[a:a + 1, :] + t2_ref[0:nb, :]
        flat.append(lax.broadcasted_iota(jnp.int32, (nb, tm), 0).astype(F32) + float(a * k))
        off += nb
    n_cand = -(-off // 8) * 8
    s_ref[off:n_cand, :] = jnp.full((n_cand - off, tm), -jnp.inf, F32)
    flat.append(jnp.full((n_cand - off, tm), float(k * k), F32))
    _topk_rows([(s_ref, tc_ref, ic_ref)], jnp.concatenate(flat, axis=0), k)
    jc = ic_ref[...]
    ja = jnp.floor(jc * (1.0 / k))
    jb = jc - ja * k
    e1 = jnp.zeros(jc.shape, F32)
    e2 = jnp.zeros(jc.shape, F32)
    for a in range(k):
        e1 = e1 + jnp.where(ja == float(a), i1_ref[a:a + 1, :], 0.0)
        e2 = e2 + jnp.where(jb == float(a), i2_ref[a:a + 1, :], 0.0)
    top = tc_ref[...]
    e = jnp.exp(top - top[0:1, :])
    a_ref[...] = e1
    b_ref[...] = e2
    gate_ref[...] = e / jnp.sum(e, axis=0, keepdims=True)


def _peer_select(x, g, shift, scale, wqt_hi, wqt_lo, sub_keys, tm, tiles_per_seg):
    n, d = x.shape
    k = PEER_TOPK
    nq = PEER_HEADS * PEER_DK
    out_spec = pl.BlockSpec((k, tm), lambda i, h: (h, i))
    out_shape = jax.ShapeDtypeStruct((PEER_HEADS * k, n), F32)
    mod_spec = lambda mod: pl.BlockSpec((None, mod.shape[1], d), lambda i, h: (i // tiles_per_seg, 0, 0))
    return pl.pallas_call(
        _peer_select_kernel,
        grid=(n // tm, PEER_HEADS),
        in_specs=[
            pl.BlockSpec((tm, d), lambda i, h: (i, 0)),
            pl.BlockSpec((1, d), lambda i, h: (0, 0)),
            mod_spec(shift),
            mod_spec(scale),
            pl.BlockSpec((nq, d), lambda i, h: (0, 0)),
            pl.BlockSpec((nq, d), lambda i, h: (0, 0)),
            pl.BlockSpec((2, PEER_NKEYS, PEER_DK // 2), lambda i, h: (0, 0, 0)),
        ],
        out_specs=[out_spec] * 3,
        out_shape=[out_shape] * 3,
        scratch_shapes=[pltpu.VMEM((nq, tm), F32)] + [pltpu.VMEM((PEER_NKEYS, tm), F32)] * 2
        + [pltpu.VMEM((k, tm), F32)] * 6,
        compiler_params=_cparams(("parallel", "arbitrary")),
        name="peer_select",
    )(x, g.reshape(1, d), shift, scale, wqt_hi, wqt_lo, sub_keys)


def _peer_dense_kernel(x_ref, g_ref, sh_ref, sc_ref, gt_ref, a_ref, b_ref, w_ref, u_ref, v_ref, out_ref,
                       f_ref, gs_ref, at_ref, bt_ref, wt_ref, p_ref, acc_ref):
    e = pl.program_id(1)
    tc = x_ref.shape[0]
    te = u_ref.shape[0]
    nk = PEER_NKEYS
    half = nk // 2
    npair = te // (2 * nk)
    hi_mask = jnp.int32(-65536)

    @pl.when(e == 0)
    def _():
        f_ref[...] = _norm_mod(x_ref[...], g_ref[...], sh_ref[...], sc_ref[...]).astype(BF16)
        acc_ref[...] = jnp.zeros(acc_ref.shape, F32)
        at_ref[...] = a_ref[...].T
        bt_ref[...] = b_ref[...].T
        wt_ref[...] = w_ref[...].T
        key = lax.broadcasted_iota(jnp.int32, (nk, nk), 0).astype(F32)

        def build(c, carry):
            first = jnp.where(key == at_ref[pl.ds(c, 1), :], 1.0, 0.0).astype(BF16)
            second = jnp.where(key == bt_ref[pl.ds(c, 1), :], wt_ref[pl.ds(c, 1), :], 0.0).astype(BF16)
            g = pltpu.bitcast(_dot_nt(first, second), jnp.int32) + 0x8000
            packed = (g[:half] & hi_mask) | lax.shift_right_logical(g[half:], jnp.full((half, nk), 16, jnp.int32))
            for k in range(half // 8):
                gs_ref[pl.ds(pl.multiple_of((k * tc + c) * 8, 8), 8), :] = packed[k * 8:(k + 1) * 8, :]
            return carry

        lax.fori_loop(0, tc, build, 0, unroll=32)

    for r in range(npair):
        p = e * npair + r
        packed = gs_ref[pl.ds((p // 8) * (tc * 8) + p % 8, tc, stride=8), :]
        g_a = pltpu.bitcast(packed & hi_mask, F32)
        g_b = pltpu.bitcast(packed << 16, F32)
        act = _dot_nt(f_ref[...], u_ref[r * 2 * nk:(r + 1) * 2 * nk, :])
        gelu = 0.5 * act * (1.0 + lax.erf(act * (2.0 ** -0.5)))
        p_ref[:, r * 2 * nk:r * 2 * nk + nk] = (g_a * gelu[:, :nk]).astype(BF16)
        p_ref[:, r * 2 * nk + nk:(r + 1) * 2 * nk] = (g_b * gelu[:, nk:]).astype(BF16)
    acc_ref[...] += _dot(p_ref[...], v_ref[...])

    @pl.when(e == pl.num_programs(1) - 1)
    def _():
        out_ref[...] = x_ref[...] + gt_ref[...] * acc_ref[...]


def _pair_permute(tab):
    half = PEER_NKEYS // 2
    t = tab.reshape(tab.shape[:-2] + (2, half, PEER_NKEYS, tab.shape[-1]))
    return jnp.swapaxes(t, -4, -3).reshape(tab.shape).astype(BF16)


def _peer_dense(x, g, shift, scale, gate, first, second, weight, u_tab, v_tab, tc, te, tiles_per_seg):
    n, d = x.shape
    n_exp = u_tab.shape[0]
    nhk = PEER_HEADS * PEER_TOPK
    mod_spec = lambda mod: pl.BlockSpec((None, mod.shape[1], d), lambda i, e: (i // tiles_per_seg, 0, 0))
    sel_spec = pl.BlockSpec((nhk, tc), lambda i, e: (0, i))
    return pl.pallas_call(
        _peer_dense_kernel,
        grid=(n // tc, n_exp // te),
        in_specs=[
            pl.BlockSpec((tc, d), lambda i, e: (i, 0)),
            pl.BlockSpec((1, d), lambda i, e: (0, 0)),
            mod_spec(shift), mod_spec(scale), mod_spec(gate),
            sel_spec, sel_spec, sel_spec,
            pl.BlockSpec((te, d), lambda i, e: (e, 0)),
            pl.BlockSpec((te, d), lambda i, e: (e, 0)),
        ],
        out_specs=pl.BlockSpec((tc, d), lambda i, e: (i, 0)),
        out_shape=jax.ShapeDtypeStruct((n, d), F32),
        scratch_shapes=[
            pltpu.VMEM((tc, d), BF16),
            pltpu.VMEM((tc * PEER_NKEYS // 2, PEER_NKEYS), jnp.int32),
            pltpu.VMEM((tc, nhk), F32),
            pltpu.VMEM((tc, nhk), F32),
            pltpu.VMEM((tc, nhk), F32),
            pltpu.VMEM((tc, te), BF16),
            pltpu.VMEM((tc, d), F32),
        ],
        compiler_params=_cparams(("parallel", "arbitrary")),
        name="peer_dense",
    )(x, g.reshape(1, d), shift, scale, gate, first, second, weight, u_tab, v_tab)


def _final_norm_kernel(x_ref, g_ref, o_ref):
    x = x_ref[...]
    ms = jnp.mean(x * x, axis=-1, keepdims=True)
    o_ref[...] = x * lax.rsqrt(ms + RMS_EPS) * g_ref[...]


def _final_norm(x, g, tm):
    n, d = x.shape
    return pl.pallas_call(
        _final_norm_kernel,
        grid=(n // tm,),
        in_specs=[pl.BlockSpec((tm, d), lambda i: (i, 0)), pl.BlockSpec((1, d), lambda i: (0, 0))],
        out_specs=pl.BlockSpec((tm, d), lambda i: (i, 0)),
        out_shape=jax.ShapeDtypeStruct((n, d), F32),
        compiler_params=_cparams(("parallel",)),
        name="final_norm",
    )(x, g.reshape(1, d))


TM = 256
PEER_TC = 512
PEER_TE = 2048


def _split_w(w):
    hi = w.astype(BF16)
    return hi, (w - hi.astype(F32)).astype(BF16)


def kernel(x_prompt, x_sample, cache_k_moba, cache_v_moba, state_kv_dil0, state_kv_dil1, state_kv_dil2, page_table,
           c_prompt, c_sample, w_ada, b_ada, g_norm, w_qkv, w_o, rel_bias, w_pq, peer_sub_keys, peer_u, peer_v,
           g_final):
    B, S, D = x_prompt.shape
    DB, T, _ = x_sample.shape
    n_a, n_pool = cache_k_moba.shape[:2]
    n_pages = page_table.shape[1]
    past_len = n_pages * PAGE_SIZE
    assert D == D_MODEL and S % MOBA_BLOCK == 0 and past_len % MOBA_BLOCK == 0 and S % TM == 0
    dil_states = (state_kv_dil0, state_kv_dil1, state_kv_dil2)
    for st, (win, dil) in zip(dil_states, DIL_GROUPS):
        assert st.shape[2] == win and win // dil == DIL_CHUNK and S % (dil * DIL_CHUNK) == 0

    np_, ns_ = B * S, DB * T
    tms = min(TM, ns_)
    hp = x_prompt.reshape(np_, D)
    hs = x_sample.reshape(ns_, D)

    mod_all = _ada(jnp.concatenate([c_prompt, c_sample], axis=0), w_ada, b_ada)
    wqkv_bf, wo_bf = w_qkv.astype(BF16), w_o.astype(BF16)
    wpqt_hi, wpqt_lo = _split_w(jnp.swapaxes(w_pq, 1, 2))
    u_bf, v_bf = _pair_permute(peer_u), _pair_permute(peer_v)
    pool_kt = cache_k_moba.transpose(0, 1, 3, 4, 2).reshape(n_a, n_pool, ATTN_DIM, PAGE_SIZE)
    pool_vt = cache_v_moba.transpose(0, 1, 3, 4, 2).reshape(n_a, n_pool, ATTN_DIM, PAGE_SIZE)

    bias_d = _bias_by_distance(rel_bias)
    moba_bias_p = _moba_prompt_bias(bias_d, S)
    moba_bias_s = _moba_sample_bias(bias_d, past_len, T)
    dil_bias_p = [_dil_prompt_bias(bias_d, g, w, d) for g, (w, d) in enumerate(DIL_GROUPS)]
    dil_bias_s = [_dil_sample_bias(bias_d, g, w, d, dil_states[g].shape[2], T)
                  for g, (w, d) in enumerate(DIL_GROUPS)]

    states_t = [st.transpose(0, 1, 3, 4, 5, 2).reshape(st.shape[0], DB, 2 * GROUP_DIM, st.shape[2])
                for st in dil_states]

    kp_rows, vp_rows, ks_rows, vs_rows = [], [], [], []
    dil_p = [[] for _ in DIL_GROUPS]
    dil_s = [None for _ in DIL_GROUPS]
    for i in range(DEPTH):
        j = i // 2
        mod = mod_all[i].reshape(B + DB, 6, D)
        mp = [mod[:B, c].reshape(B, 1, D) for c in range(6)]
        ms = [jnp.repeat(mod[B:, c], T, axis=0).reshape(ns_ // tms, tms, D) for c in range(6)]
        seg_p, seg_s = S // TM, 1

        qkv_p, kt_p, vt_p = _nmm(hp, g_norm[i, 0], mp[0], mp[1], wqkv_bf[i], TM, seg_p, kv_transposed=True)
        qkv_s = _nmm(hs, g_norm[i, 0], ms[0], ms[1], wqkv_bf[i], tms, seg_s)
        qkv_p3 = qkv_p.reshape(B, S, QKV_DIM)
        qkv_s3 = qkv_s.reshape(DB, T, QKV_DIM)
        qkv_s5 = qkv_s.reshape(DB, T, 3, N_HEADS, HEAD_DIM)
        if i % 2 == 0:
            attn_p = _moba_prompt(qkv_p3, moba_bias_p).reshape(np_, ATTN_DIM)
            attn_s = _moba_sample(qkv_s3, pool_kt, pool_vt, page_table, moba_bias_s, j, n_pages).reshape(ns_, ATTN_DIM)
            rows = lambda t: t.reshape(B, N_HEADS, HEAD_DIM, S).transpose(0, 3, 1, 2)
            kp_rows.append(rows(kt_p))
            vp_rows.append(rows(vt_p))
            ks_rows.append(qkv_s5[:, :, 1])
            vs_rows.append(qkv_s5[:, :, 2])
        else:
            op, lp, os_, ls = [], [], [], []
            for g, (win, dil) in enumerate(DIL_GROUPS):
                o, lse = _dil_prompt(qkv_p3, dil_bias_p[g], g, dil)
                op += [a.reshape(np_, a.shape[-1]) for a in o]
                lp += [a.reshape(np_, a.shape[-1]) for a in lse]
                o, lse, dil_s[g] = _dil_sample(qkv_s3, states_t[g], dil_s[g], *dil_bias_s[g], j, g)
                os_.append(o.reshape(ns_, GROUP_DIM))
                ls.append(lse.reshape(ns_, GROUP_DIM))
                keep = min(win, S)
                kv_p = jnp.stack([t[:, g * GROUP_DIM:(g + 1) * GROUP_DIM, S - keep:] for t in (kt_p, vt_p)], axis=1)
                dil_p[g].append(kv_p.reshape(B, 2, GROUP_HEADS, HEAD_DIM, keep).transpose(0, 4, 1, 2, 3))
            attn_p, attn_s = (op, lp), (os_, ls)
        hp = _oproj(hp, attn_p, mp[2], wo_bf[i], TM, seg_p)
        hs = _oproj(hs, attn_s, ms[2], wo_bf[i], tms, seg_s)

        sel_p = _peer_select(hp, g_norm[i, 1], mp[3], mp[4], wpqt_hi[i], wpqt_lo[i], peer_sub_keys[i], TM, seg_p)
        sel_s = _peer_select(hs, g_norm[i, 1], ms[3], ms[4], wpqt_hi[i], wpqt_lo[i], peer_sub_keys[i], tms, seg_s)
        hp = _peer_dense(hp, g_norm[i, 1], mp[3], mp[4], mp[5], *sel_p, u_bf[i], v_bf[i], PEER_TC, PEER_TE,
                         S // PEER_TC)
        hs = _peer_dense(hs, g_norm[i, 1], ms[3], ms[4], ms[5], *sel_s, u_bf[i], v_bf[i], tms, PEER_TE, 1)

    y_prompt = _final_norm(hp, g_final, TM).reshape(B, S, D)
    y_sample = _final_norm(hs, g_final, tms).reshape(DB, T, D)
    return (y_prompt, y_sample,
            jnp.stack(kp_rows), jnp.stack(vp_rows), jnp.stack(ks_rows), jnp.stack(vs_rows),
            jnp.stack(dil_p[0]), jnp.stack(dil_p[1]), jnp.stack(dil_p[2]),
            *[st.reshape(st.shape[:2] + (2, GROUP_HEADS, HEAD_DIM, st.shape[3])).transpose(0, 1, 5, 2, 3, 4)
              for st in dil_s])
```

```python
import functools
import math

import numpy as np
import jax
import jax.numpy as jnp
from jax import lax
from jax.experimental import pallas as pl
from jax.experimental.pallas import tpu as pltpu

F32 = jnp.float32
BF16 = jnp.bfloat16
NEG = -1e30

D_MODEL = 1024
N_HEADS = 12
HEAD_DIM = 64
ATTN_DIM = N_HEADS * HEAD_DIM
QKV_DIM = 3 * ATTN_DIM
ATTN_SCALE = HEAD_DIM ** -0.5
DEPTH = 4
PAGE_SIZE = 128
MOBA_BLOCK = 256
MOBA_TOPK = 3
DIL_GROUPS = ((128, 1), (512, 4), (2048, 16))
GROUP_HEADS = 4
GROUP_DIM = GROUP_HEADS * HEAD_DIM
DIL_CHUNK = 128
REL_BUCKETS = 32
REL_MAX_DIST = 2048
PEER_HEADS = 8
PEER_NKEYS = 128
PEER_DK = 256
PEER_TOPK = 16
RMS_EPS = 1e-6

VMEM_LIMIT = 56 * 1024 * 1024


def _cparams(sem):
    return pltpu.CompilerParams(dimension_semantics=sem, vmem_limit_bytes=VMEM_LIMIT)


def _split(x):
    hi = x.astype(BF16)
    lo = (x - hi.astype(F32)).astype(BF16)
    return hi, lo


def _dot(a, b):
    return jnp.dot(a, b, preferred_element_type=F32)


def _dot_nt(a, b):
    return lax.dot_general(a, b, (((1,), (1,)), ((), ())), preferred_element_type=F32)


def _dot3(a, b_hi, b_lo):
    a_hi, a_lo = _split(a)
    return _dot(a_hi, b_hi) + _dot(a_lo, b_hi) + _dot(a_hi, b_lo)


def _dot3_nt(a, b):
    a_hi, a_lo = _split(a)
    b_hi, b_lo = _split(b)
    return _dot_nt(a_hi, b_hi) + _dot_nt(a_lo, b_hi) + _dot_nt(a_hi, b_lo)


def _norm_mod(x, g, shift, scale):
    ms = jnp.mean(x * x, axis=-1, keepdims=True)
    xn = x * lax.rsqrt(ms + RMS_EPS) * g
    return xn * (1.0 + scale) + shift


def _bucket_np(dist):
    n = np.maximum(dist, 0)
    max_exact = REL_BUCKETS // 2
    nf = np.maximum(n, 1).astype(np.float64)
    large = max_exact + (np.log(nf / max_exact) / math.log(REL_MAX_DIST / max_exact)
                         * (REL_BUCKETS - max_exact)).astype(np.int64)
    large = np.minimum(large, REL_BUCKETS - 1)
    return np.where(n < max_exact, n, large).astype(np.int32)


BIAS_PAD = 256
BIAS_LEN = 4352


def _bias_by_distance(rel_bias):
    onehot = np.zeros((BIAS_LEN, REL_BUCKETS), np.float32)
    onehot[np.arange(BIAS_LEN), _bucket_np(np.arange(BIAS_LEN))] = 1.0
    by_dist = jnp.dot(jnp.asarray(onehot), rel_bias, precision=lax.Precision.HIGHEST)
    return jnp.concatenate([jnp.full((N_HEADS, BIAS_PAD), NEG, F32), by_dist.T], axis=1)


def _toeplitz(wp, rows, cols):
    n2 = wp.shape[-1]
    x = jnp.tile(wp, (1,) * (wp.ndim - 1) + (rows,))[..., :rows * (n2 - 1)]
    return x.reshape(wp.shape[:-1] + (rows, n2 - 1))[..., :cols]


def _rank_select(s, lane_n, n_valid, topk):
    nb = s.shape[1]
    rank = jnp.zeros(s.shape, F32)
    for m in range(nb):
        sm = s[:, m:m + 1]
        beats = (sm > s) | ((sm == s) & (m < lane_n))
        beats = beats & (m < n_valid)
        rank = rank + jnp.where(beats, 1.0, 0.0)
    sel = (lane_n < n_valid) & (rank < float(topk))
    return jnp.where(sel, 1.0, 0.0)


def _rank_select_rows(s, row_n, n_valid, topk):
    nb = s.shape[0]
    rank = jnp.zeros(s.shape, F32)
    for m in range(nb):
        sm = s[m:m + 1, :]
        beats = (sm > s) | ((sm == s) & (m < row_n))
        beats = beats & (m < n_valid)
        rank = rank + jnp.where(beats, 1.0, 0.0)
    sel = (row_n < n_valid) & (rank < float(topk))
    return jnp.where(sel, 1.0, 0.0)


def _ada_kernel(c_ref, w_ref, b_ref, o_ref):
    c = c_ref[...]
    s = c * (1.0 / (1.0 + jnp.exp(-c)))
    w_hi, w_lo = _split(w_ref[...])
    o_ref[...] = _dot3(s, w_hi, w_lo) + b_ref[...]


def _ada(c_all, w_ada, b_ada):
    n = c_all.shape[0]
    depth, d, n6 = w_ada.shape
    tn = 1536
    return pl.pallas_call(
        _ada_kernel,
        grid=(depth, n6 // tn),
        in_specs=[
            pl.BlockSpec((n, d), lambda i, j: (0, 0)),
            pl.BlockSpec((None, d, tn), lambda i, j: (i, 0, j)),
            pl.BlockSpec((None, 1, tn), lambda i, j: (i, 0, j)),
        ],
        out_specs=pl.BlockSpec((None, n, tn), lambda i, j: (i, 0, j)),
        out_shape=jax.ShapeDtypeStruct((depth, n, n6), F32),
        compiler_params=_cparams(("parallel", "parallel")),
        name="ada",
    )(c_all, w_ada, b_ada.reshape(depth, 1, n6))


def _nmm_kernel(x_ref, g_ref, sh_ref, sc_ref, w_ref, o_ref, *kvt_refs):
    f = _norm_mod(x_ref[...], g_ref[...], sh_ref[...], sc_ref[...])
    out = _dot(f.astype(BF16), w_ref[...])
    o_ref[...] = out
    for part, t_ref in enumerate(kvt_refs, start=1):
        t_ref[...] = out[:, part * ATTN_DIM:(part + 1) * ATTN_DIM].T


def _mod_spec(mod, tm, tiles_per_seg):
    r = mod.shape[1]
    return pl.BlockSpec((None, r, D_MODEL), lambda i, *_: (i // tiles_per_seg, 0, 0))


def _nmm(x, g, shift, scale, w, tm, tiles_per_seg, kv_transposed=False):
    n, d = x.shape
    nout = w.shape[1]
    out_specs = [pl.BlockSpec((tm, nout), lambda i: (i, 0))]
    out_shape = [jax.ShapeDtypeStruct((n, nout), F32)]
    if kv_transposed:
        n_seq = n // (tm * tiles_per_seg)
        t_spec = pl.BlockSpec((None, ATTN_DIM, tm), lambda i: (i // tiles_per_seg, 0, i % tiles_per_seg))
        out_specs += [t_spec, t_spec]
        out_shape += [jax.ShapeDtypeStruct((n_seq, ATTN_DIM, tm * tiles_per_seg), F32)] * 2
    res = pl.pallas_call(
        _nmm_kernel,
        grid=(n // tm,),
        in_specs=[
            pl.BlockSpec((tm, d), lambda i: (i, 0)),
            pl.BlockSpec((1, d), lambda i: (0, 0)),
            _mod_spec(shift, tm, tiles_per_seg),
            _mod_spec(scale, tm, tiles_per_seg),
            pl.BlockSpec((d, nout), lambda i: (0, 0)),
        ],
        out_specs=out_specs,
        out_shape=out_shape,
        compiler_params=_cparams(("parallel",)),
        name="nmm",
    )(x, g.reshape(1, d), shift, scale, w)
    return res if kv_transposed else res[0]


def _moba_prompt_kernel(q_ref, k_ref, v_ref, bias_ref, o_ref, kmean_ref, mask_ref):
    i = pl.program_id(2)
    blk = MOBA_BLOCK
    nb = k_ref.shape[0] // blk

    @pl.when(i == 0)
    def _():
        for n in range(nb):
            kmean_ref[n:n + 1, :] = jnp.mean(k_ref[n * blk:(n + 1) * blk, :], axis=0, keepdims=True)

    lane_head = lax.broadcasted_iota(jnp.int32, (blk, 128), 1) // HEAD_DIM
    row_n = lax.broadcasted_iota(jnp.int32, (nb, blk), 0)
    km_head = lax.broadcasted_iota(jnp.int32, (nb, 128), 1) // HEAD_DIM
    q2 = q_ref[...] * ATTN_SCALE
    qb = []
    for hh in range(2):
        q = jnp.where(lane_head == hh, q2, 0.0)
        scores = _dot3_nt(jnp.where(km_head == hh, kmean_ref[...], 0.0), q)
        sel = _rank_select_rows(scores, row_n, i, MOBA_TOPK)
        sel = jnp.concatenate([sel, jnp.zeros((128 - nb, blk), F32)], axis=0).T
        for n in range(nb):
            mask_ref[hh, n] = jnp.broadcast_to(sel[:, n:n + 1], (blk, 128))
        qb.append(q.astype(BF16))

    def tile(j, hh, masked):
        r0 = pl.multiple_of(j * blk, blk)
        logits = _dot_nt(qb[hh], k_ref[pl.ds(r0, blk), :].astype(BF16)) + bias_ref[i - j, hh]
        if masked:
            keep = mask_ref[hh, j] > 0.5
            logits = jnp.where(jnp.concatenate([keep] * (blk // 128), axis=1), logits, NEG)
        return logits, v_ref[pl.ds(r0, blk), :].astype(BF16)

    state = []
    for hh in range(2):
        logits, vb = tile(i, hh, False)
        m = jnp.max(logits, axis=1, keepdims=True)
        p = jnp.exp(logits - m)
        state += [m, jnp.sum(p, axis=1, keepdims=True), _dot(p.astype(BF16), vb)]

    def body(u, carry):
        out = []
        for hh in range(2):
            m, l, acc = carry[3 * hh:3 * hh + 3]
            lg0, v0 = tile(2 * u, hh, True)
            lg1, v1 = tile(2 * u + 1, hh, True)
            m_new = jnp.maximum(m, jnp.maximum(jnp.max(lg0, axis=1, keepdims=True), jnp.max(lg1, axis=1, keepdims=True)))
            alpha = jnp.exp(m - m_new)
            p0 = jnp.exp(lg0 - m_new)
            p1 = jnp.exp(lg1 - m_new)
            l = alpha * l + jnp.sum(p0, axis=1, keepdims=True) + jnp.sum(p1, axis=1, keepdims=True)
            acc = alpha * acc + _dot(p0.astype(BF16), v0) + _dot(p1.astype(BF16), v1)
            out += [m_new, l, acc]
        return tuple(out)

    _, l0, acc0, _, l1, acc1 = lax.fori_loop(0, (i + 1) // 2, body, tuple(state))
    o_ref[...] = jnp.where(lane_head == 0, acc0 / l0, acc1 / l1)


def _moba_prompt(qkv, bias):
    b, s, _ = qkv.shape
    blk = MOBA_BLOCK
    nb = s // blk
    npair = N_HEADS // 2
    return pl.pallas_call(
        _moba_prompt_kernel,
        grid=(npair, b, nb),
        in_specs=[
            pl.BlockSpec((None, blk, 128), lambda hp, bb, i: (bb, i, hp)),
            pl.BlockSpec((None, s, 128), lambda hp, bb, i: (bb, 0, npair + hp)),
            pl.BlockSpec((None, s, 128), lambda hp, bb, i: (bb, 0, 2 * npair + hp)),
            pl.BlockSpec((nb, 2, blk, blk), lambda hp, bb, i: (0, hp, 0, 0)),
        ],
        out_specs=pl.BlockSpec((None, blk, 128), lambda hp, bb, i: (bb, i, hp)),
        out_shape=jax.ShapeDtypeStruct((b, s, ATTN_DIM), F32),
        scratch_shapes=[pltpu.VMEM((nb, 128), F32), pltpu.VMEM((2, nb, blk, 128), F32)],
        compiler_params=_cparams(("parallel", "parallel", "arbitrary")),
        name="moba_prompt",
    )(qkv, qkv, qkv, bias)


def _moba_prompt_bias(bias_d, s):
    blk = MOBA_BLOCK
    nb = s // blk
    assert blk <= BIAS_PAD
    wps = []
    for delta in range(nb):
        z = BIAS_PAD + delta * blk
        lo = bias_d[:, z - blk:z + 1][:, ::-1]
        hi = bias_d[:, z + 1:z + blk][:, ::-1]
        wps.append(jnp.concatenate([lo, hi], axis=1))
    return _toeplitz(jnp.stack(wps), blk, blk)


def _moba_sample_kernel(pt_ref, new_ref, *refs, n_pages, t_new):
    del pt_ref
    kt_refs, vt_refs = refs[:n_pages], refs[n_pages:2 * n_pages]
    sbias_ref, o_ref, lg_ref, p_ref = refs[2 * n_pages:]
    rows = N_HEADS * t_new
    pages_per_block = MOBA_BLOCK // PAGE_SIZE
    n_blocks = n_pages // pages_per_block
    r_head = lax.broadcasted_iota(jnp.int32, (rows, ATTN_DIM), 0) // t_new
    c_head = lax.broadcasted_iota(jnp.int32, (rows, ATTN_DIM), 1) // HEAD_DIM
    pad = jnp.zeros((PAGE_SIZE - t_new, ATTN_DIM), F32)

    q = new_ref[:, 0:ATTN_DIM] * ATTN_SCALE
    qhi, qlo = _split(jnp.where(r_head == c_head, jnp.concatenate([q] * N_HEADS, axis=0), 0.0))

    scores = []
    for n in range(n_blocks):
        ksum = None
        for pg in range(n * pages_per_block, (n + 1) * pages_per_block):
            kt = kt_refs[pg][...]
            lg_ref[pg] = _dot(qhi, kt.astype(BF16))
            ksum = kt if ksum is None else ksum + kt
        k_hi, k_lo = _split(ksum)
        y = _dot(qhi, k_hi) + _dot(qlo, k_hi) + _dot(qhi, k_lo)
        scores.append(jnp.sum(y, axis=1, keepdims=True))
    s = jnp.concatenate(scores, axis=1)
    lane_n = lax.broadcasted_iota(jnp.int32, (rows, n_blocks), 1)
    sel = _rank_select(s, lane_n, n_blocks, MOBA_TOPK)
    knew = jnp.concatenate([new_ref[:, ATTN_DIM:2 * ATTN_DIM], pad], axis=0)
    lg_ref[n_pages] = _dot_nt(qhi, knew.astype(BF16))

    mrun = jnp.full((rows, PAGE_SIZE), NEG, F32)
    for pg in range(n_pages + 1):
        lgt = lg_ref[pg] + sbias_ref[pg]
        if pg < n_pages:
            n = pg // pages_per_block
            lgt = jnp.where(sel[:, n:n + 1] > 0.5, lgt, NEG)
        lg_ref[pg] = lgt
        mrun = jnp.maximum(mrun, lgt)
    m = jnp.max(mrun, axis=1, keepdims=True)
    srun = jnp.zeros((rows, PAGE_SIZE), F32)
    for pg in range(n_pages + 1):
        e = jnp.exp(lg_ref[pg] - m)
        srun = srun + e
        p_ref[pg] = e.astype(BF16)
    linv = 1.0 / jnp.sum(srun, axis=1, keepdims=True)

    vnew = jnp.concatenate([new_ref[:, 2 * ATTN_DIM:3 * ATTN_DIM], pad], axis=0)
    acc = _dot(p_ref[n_pages], vnew.astype(BF16))
    for pg in range(n_pages):
        acc = acc + _dot_nt(p_ref[pg], vt_refs[pg][...].astype(BF16))
    acc = jnp.where(r_head == c_head, acc * linv, 0.0)
    out = acc[0:t_new, :]
    for h in range(1, N_HEADS):
        out = out + acc[h * t_new:(h + 1) * t_new, :]
    o_ref[...] = out


def _moba_sample(qkv, pool_kt, pool_vt, page_table, sbias, layer, n_pages):
    db, t, _ = qkv.shape
    rows = N_HEADS * t
    page_spec = lambda pg: pl.BlockSpec((None, None, ATTN_DIM, PAGE_SIZE), lambda b, pt: (layer, pt[b, pg], 0, 0))
    grid_spec = pltpu.PrefetchScalarGridSpec(
        num_scalar_prefetch=1,
        grid=(db,),
        in_specs=[pl.BlockSpec((None, t, QKV_DIM), lambda b, pt: (b, 0, 0))]
        + [page_spec(pg) for pg in range(n_pages)] * 2
        + [pl.BlockSpec((n_pages + 1, rows, PAGE_SIZE), lambda b, pt: (0, 0, 0))],
        out_specs=pl.BlockSpec((None, t, ATTN_DIM), lambda b, pt: (b, 0, 0)),
        scratch_shapes=[
            pltpu.VMEM((n_pages + 1, rows, PAGE_SIZE), F32),
            pltpu.VMEM((n_pages + 1, rows, PAGE_SIZE), BF16),
        ],
    )
    return pl.pallas_call(
        functools.partial(_moba_sample_kernel, n_pages=n_pages, t_new=t),
        grid_spec=grid_spec,
        out_shape=jax.ShapeDtypeStruct((db, t, ATTN_DIM), F32),
        compiler_params=_cparams(("parallel",)),
        name="moba_sample",
    )(page_table, qkv, *([pool_kt] * n_pages), *([pool_vt] * n_pages), sbias)


def _moba_sample_bias(bias_d, past_len, t):
    n_pages = past_len // PAGE_SIZE
    z = BIAS_PAD
    assert PAGE_SIZE <= BIAS_PAD and past_len + t <= BIAS_LEN
    past = jnp.stack([bias_d[:, z + tq + 1:z + tq + 1 + past_len][:, ::-1] for tq in range(t)], axis=1)
    past = past.reshape(N_HEADS * t, n_pages, PAGE_SIZE).transpose(1, 0, 2)
    new = jnp.stack([bias_d[:, z + tq - (PAGE_SIZE - 1):z + tq + 1][:, ::-1] for tq in range(t)], axis=1)
    return jnp.concatenate([past, new.reshape(1, N_HEADS * t, PAGE_SIZE)], axis=0)


def _dil_prompt_kernel(*refs, dil):
    q_refs, k_refs, v_refs = refs[0:2], refs[2:4], refs[4:6]
    bias_ref = refs[6]
    o_refs, lse_refs = refs[7:9], refs[9:11]
    r, c = pl.program_id(1), pl.program_id(2)
    ch = DIL_CHUNK
    rows = lambda chunk: pl.ds(r + dil * ch * chunk, ch, stride=dil) if dil > 1 else pl.ds(ch * chunk, ch)
    cur, prev = rows(c), rows(jnp.maximum(c - 1, 0))
    col = lax.broadcasted_iota(jnp.int32, (ch, 2 * ch), 1)
    keep = (col >= ch) | (c > 0)
    lane_head = lax.broadcasted_iota(jnp.int32, (ch, 2 * HEAD_DIM), 1) // HEAD_DIM
    for pc in range(2):
        q = q_refs[pc][cur, :] * ATTN_SCALE
        k = jnp.concatenate([k_refs[pc][prev, :], k_refs[pc][cur, :]], axis=0).astype(BF16)
        v = jnp.concatenate([v_refs[pc][prev, :], v_refs[pc][cur, :]], axis=0).astype(BF16)
        o_h, lse_h = [], []
        for hh in range(2):
            qh = jnp.where(lane_head == hh, q, 0.0).astype(BF16)
            s = jnp.where(keep, _dot_nt(qh, k) + bias_ref[2 * pc + hh], NEG)
            m = jnp.max(s, axis=1, keepdims=True)
            p = jnp.exp(s - m)
            l = jnp.sum(p, axis=1, keepdims=True)
            o_h.append(_dot(p.astype(BF16), v) / l)
            lse_h.append(m + jnp.log(l))
        o_refs[pc][cur, :] = jnp.where(lane_head == 0, o_h[0], o_h[1])
        lse_refs[pc][cur, :] = jnp.where(lane_head == 0, lse_h[0], lse_h[1])


def _dil_prompt(qkv, bias, g, dil):
    b, s, _ = qkv.shape
    nch = s // dil // DIL_CHUNK
    pieces = ATTN_DIM // 128
    col = lambda part, pc: pl.BlockSpec((None, s, 128), lambda bb, r, c: (bb, 0, part * pieces + 2 * g + pc))
    out_spec = pl.BlockSpec((None, s, 128), lambda bb, r, c: (bb, 0, 0))
    res = pl.pallas_call(
        functools.partial(_dil_prompt_kernel, dil=dil),
        grid=(b, dil, nch),
        in_specs=[col(part, pc) for part in range(3) for pc in range(2)]
        + [pl.BlockSpec((GROUP_HEADS, DIL_CHUNK, 2 * DIL_CHUNK), lambda bb, r, c: (0, 0, 0))],
        out_specs=[out_spec] * 4,
        out_shape=[jax.ShapeDtypeStruct((b, s, 128), F32)] * 4,
        compiler_params=_cparams(("parallel", "arbitrary", "arbitrary")),
        name=f"dil_prompt{g}",
    )(*([qkv] * 6), bias)
    return res[0:2], res[2:4]


def _dil_prompt_bias(bias_d, g, win, dil):
    ch = DIL_CHUNK
    assert win // dil == ch
    z = BIAS_PAD
    heads = bias_d[g * GROUP_HEADS:(g + 1) * GROUP_HEADS]
    steps = heads[:, z:z + win + 1:dil][:, ::-1]
    wp = jnp.concatenate([steps, jnp.full((GROUP_HEADS, 4 * ch - (ch + 1)), NEG, F32)], axis=1)
    return _toeplitz(wp, ch, 2 * ch)


def _dil_sample_kernel(new_ref, buf_ref, bb_ref, bn_ref, *rest, g, t_new):
    o_ref, lse_ref, st_ref = rest[-3:]
    lb = buf_ref.shape[1]
    rows = GROUP_HEADS * t_new
    ch = DIL_CHUNK
    r_head = lax.broadcasted_iota(jnp.int32, (rows, GROUP_DIM), 0) // t_new
    c_head = lax.broadcasted_iota(jnp.int32, (rows, GROUP_DIM), 1) // HEAD_DIM
    diag = r_head == c_head
    pad = jnp.zeros((ch - t_new, GROUP_DIM), F32)
    qcol = g * GROUP_DIM
    kcol = ATTN_DIM + g * GROUP_DIM
    vcol = 2 * ATTN_DIM + g * GROUP_DIM
    knew, vnew = new_ref[:, kcol:kcol + GROUP_DIM], new_ref[:, vcol:vcol + GROUP_DIM]
    q = new_ref[:, qcol:qcol + GROUP_DIM] * ATTN_SCALE
    qbd = jnp.where(diag, jnp.concatenate([q] * GROUP_HEADS, axis=0), 0.0).astype(BF16)
    buf = buf_ref[...]
    kt, vt = buf[:GROUP_DIM].astype(BF16), buf[GROUP_DIM:].astype(BF16)
    lgb = _dot(qbd, kt) + bb_ref[...]
    lgn = _dot_nt(qbd, jnp.concatenate([knew, pad], axis=0).astype(BF16)) + bn_ref[...]
    m = jnp.maximum(jnp.max(lgb, axis=1, keepdims=True), jnp.max(lgn, axis=1, keepdims=True))
    pb = jnp.exp(lgb - m)
    pn = jnp.exp(lgn - m)
    l = jnp.sum(pb, axis=1, keepdims=True) + jnp.sum(pn, axis=1, keepdims=True)
    o = _dot_nt(pb.astype(BF16), vt) + _dot(pn.astype(BF16), jnp.concatenate([vnew, pad], axis=0).astype(BF16))
    o = jnp.where(diag, o / l, 0.0)
    lse = jnp.where(diag, m + jnp.log(l), 0.0)
    o_ref[...] = sum(o[h * t_new:(h + 1) * t_new] for h in range(GROUP_HEADS))
    lse_ref[...] = sum(lse[h * t_new:(h + 1) * t_new] for h in range(GROUP_HEADS))

    tail = jnp.concatenate([jnp.zeros((ch - t_new, 2 * GROUP_DIM), F32),
                            jnp.concatenate([knew, vnew], axis=1)], axis=0).T
    shifted = pltpu.roll(buf, lb - t_new, axis=1)
    lane = lax.broadcasted_iota(jnp.int32, (2 * GROUP_DIM, ch), 1)
    if lb > ch:
        st_ref[:, 0:lb - ch] = shifted[:, 0:lb - ch]
    st_ref[:, lb - ch:lb] = jnp.where(lane >= ch - t_new, tail, shifted[:, lb - ch:lb])


def _dil_sample(qkv, state_t, prev_out, bb, bn, layer, g):
    db, t, _ = qkv.shape
    nb, _, kv_dim, lb = state_t.shape
    rows = GROUP_HEADS * t
    assert t <= DIL_CHUNK <= lb
    out_spec = pl.BlockSpec((None, t, GROUP_DIM), lambda b: (b, 0, 0))
    st_spec = pl.BlockSpec((None, None, kv_dim, lb), lambda b: (layer, b, 0, 0))
    args = [qkv, state_t, bb, bn]
    in_specs = [
        pl.BlockSpec((None, t, QKV_DIM), lambda b: (b, 0, 0)),
        st_spec,
        pl.BlockSpec((rows, lb), lambda b: (0, 0)),
        pl.BlockSpec((rows, DIL_CHUNK), lambda b: (0, 0)),
    ]
    aliases = {}
    if prev_out is not None:
        args.append(prev_out)
        in_specs.append(pl.BlockSpec(memory_space=pl.ANY))
        aliases = {4: 2}
    return pl.pallas_call(
        functools.partial(_dil_sample_kernel, g=g, t_new=t),
        grid=(db,),
        in_specs=in_specs,
        out_specs=[out_spec, out_spec, st_spec],
        out_shape=[jax.ShapeDtypeStruct((db, t, GROUP_DIM), F32)] * 2 + [jax.ShapeDtypeStruct(state_t.shape, F32)],
        input_output_aliases=aliases,
        compiler_params=_cparams(("parallel",)),
        name=f"dil_sample{g}",
    )(*args)


def _dil_sample_bias(bias_d, g, win, dil, lb, t_new):
    z = BIAS_PAD
    assert DIL_CHUNK <= BIAS_PAD and lb + t_new <= BIAS_LEN
    heads = bias_d[g * GROUP_HEADS:(g + 1) * GROUP_HEADS]
    bb, bn = [], []
    pos = np.arange(lb)
    tk = np.arange(DIL_CHUNK)
    for t in range(t_new):
        dist = lb + t - pos
        vals = heads[:, z + t + 1:z + t + 1 + lb][:, ::-1]
        bb.append(jnp.where(jnp.asarray((dist % dil == 0) & (dist <= win)), vals, NEG))
        vals = heads[:, z + t - (DIL_CHUNK - 1):z + t + 1][:, ::-1]
        ok = (tk < t_new) & (t - tk >= 0) & ((t - tk) % dil == 0) & (t - tk <= win)
        bn.append(jnp.where(jnp.asarray(ok), vals, NEG))
    as_rows = lambda tabs: jnp.stack(tabs, axis=1).reshape(GROUP_HEADS * t_new, -1)
    return as_rows(bb), as_rows(bn)


def _oproj_kernel(h_ref, o_ref, gate_ref, w_ref, out_ref):
    out_ref[...] = h_ref[...] + gate_ref[...] * _dot(o_ref[...].astype(BF16), w_ref[...])


def _oproj_dil_kernel(h_ref, *refs):
    n_groups = len(DIL_GROUPS)
    gate_ref, w_ref, out_ref = refs[-3:]
    npc = (len(refs) - 3) // (2 * n_groups)
    o_refs, l_refs = refs[:n_groups * npc], refs[n_groups * npc:2 * n_groups * npc]
    scaled = [None] * (n_groups * npc)
    for pc in range(npc):
        lses = [l_refs[g * npc + pc][...] for g in range(n_groups)]
        mx = functools.reduce(jnp.maximum, lses)
        es = [jnp.exp(l - mx) for l in lses]
        inv = 1.0 / sum(es[1:], es[0])
        for g in range(n_groups):
            scaled[g * npc + pc] = o_refs[g * npc + pc][...] * (es[g] * inv)
    o = jnp.concatenate(scaled, axis=1)
    out_ref[...] = h_ref[...] + gate_ref[...] * _dot(o.astype(BF16), w_ref[...])


def _oproj(h, attn, gate, w, tm, tiles_per_seg):
    n, d = h.shape
    row = lambda w: pl.BlockSpec((tm, w), lambda i: (i, 0))
    if isinstance(attn, tuple):
        kern, attn_args = _oproj_dil_kernel, list(attn[0]) + list(attn[1])
        attn_specs = [row(a.shape[1]) for a in attn_args]
    else:
        kern, attn_args, attn_specs = _oproj_kernel, [attn], [row(ATTN_DIM)]
    return pl.pallas_call(
        kern,
        grid=(n // tm,),
        in_specs=[row(d)] + attn_specs + [
            _mod_spec(gate, tm, tiles_per_seg),
            pl.BlockSpec((ATTN_DIM, d), lambda i: (0, 0)),
        ],
        out_specs=row(d),
        out_shape=jax.ShapeDtypeStruct((n, d), F32),
        compiler_params=_cparams(("parallel",)),
        name="oproj",
    )(h, *attn_args, gate, w)


def _topk_rows(jobs, ridx, k):
    nrows = ridx.shape[0]

    def body(it, carry):
        for src_ref, val_ref, idx_ref in jobs:
            a = src_ref[0:nrows, :]
            mx = jnp.max(a, axis=0, keepdims=True)
            ix = jnp.min(jnp.where(a == mx, ridx, 1e9), axis=0, keepdims=True)
            src_ref[0:nrows, :] = jnp.where(ridx == ix, -jnp.inf, a)
            val_ref[pl.ds(it, 1), :] = mx
            idx_ref[pl.ds(it, 1), :] = ix
        return carry

    lax.fori_loop(0, k, body, 0)


def _cand_rows(k):
    return [(a, k // (a + 1)) for a in range(k)]


def _peer_select_kernel(x_ref, g_ref, sh_ref, sc_ref, whi_ref, wlo_ref, sk_ref, a_ref, b_ref, gate_ref,
                        qt_ref, s_ref, s2_ref, t1_ref, i1_ref, t2_ref, i2_ref, tc_ref, ic_ref):
    h = pl.program_id(1)
    k = PEER_TOPK
    half = PEER_DK // 2
    tm = x_ref.shape[0]

    @pl.when(h == 0)
    def _():
        f_hi, f_lo = _split(_norm_mod(x_ref[...], g_ref[...], sh_ref[...], sc_ref[...]))
        whi = whi_ref[...]
        qt_ref[...] = _dot_nt(whi, f_hi) + _dot_nt(wlo_ref[...], f_hi) + _dot_nt(whi, f_lo)

    key_rows = lax.broadcasted_iota(jnp.int32, (PEER_NKEYS, tm), 0).astype(F32)
    row0 = pl.multiple_of(h * PEER_DK, PEER_DK)
    for part, sub_ref in enumerate((s_ref, s2_ref)):
        q_hi, q_lo = _split(qt_ref[pl.ds(row0 + part * half, half), :])
        sub_ref[...] = _dot3(sk_ref[part], q_hi, q_lo)
    _topk_rows([(s_ref, t1_ref, i1_ref), (s2_ref, t2_ref, i2_ref)], key_rows, k)
    off, flat = 0, []
    for a, nb in _cand_rows(k):
        s_ref[off:off + nb, :] = t1_ref[a:a + 1, :] + t2_ref[0:nb, :]
        flat.append(lax.broadcasted_iota(jnp.int32, (nb, tm), 0).astype(F32) + float(a * k))
        off += nb
    n_cand = -(-off // 8) * 8
    s_ref[off:n_cand, :] = jnp.full((n_cand - off, tm), -jnp.inf, F32)
    flat.append(jnp.full((n_cand - off, tm), float(k * k), F32))
    _topk_rows([(s_ref, tc_ref, ic_ref)], jnp.concatenate(flat, axis=0), k)
    jc = ic_ref[...]
    ja = jnp.floor(jc * (1.0 / k))
    jb = jc - ja * k
    e1 = jnp.zeros(jc.shape, F32)
    e2 = jnp.zeros(jc.shape, F32)
    for a in range(k):
        e1 = e1 + jnp.where(ja == float(a), i1_ref[a:a + 1, :], 0.0)
        e2 = e2 + jnp.where(jb == float(a), i2_ref[a:a + 1, :], 0.0)
    top = tc_ref[...]
    e = jnp.exp(top - top[0:1, :])
    a_ref[...] = e1
    b_ref[...] = e2
    gate_ref[...] = e / jnp.sum(e, axis=0, keepdims=True)


def _peer_select(x, g, shift, scale, wqt_hi, wqt_lo, sub_keys, tm, tiles_per_seg):
    n, d = x.shape
    k = PEER_TOPK
    nq = PEER_HEADS * PEER_DK
    out_spec = pl.BlockSpec((k, tm), lambda i, h: (h, i))
    out_shape = jax.ShapeDtypeStruct((PEER_HEADS * k, n), F32)
    mod_spec = lambda mod: pl.BlockSpec((None, mod.shape[1], d), lambda i, h: (i // tiles_per_seg, 0, 0))
    return pl.pallas_call(
        _peer_select_kernel,
        grid=(n // tm, PEER_HEADS),
        in_specs=[
            pl.BlockSpec((tm, d), lambda i, h: (i, 0)),
            pl.BlockSpec((1, d), lambda i, h: (0, 0)),
            mod_spec(shift),
            mod_spec(scale),
            pl.BlockSpec((nq, d), lambda i, h: (0, 0)),
            pl.BlockSpec((nq, d), lambda i, h: (0, 0)),
            pl.BlockSpec((2, PEER_NKEYS, PEER_DK // 2), lambda i, h: (0, 0, 0)),
        ],
        out_specs=[out_spec] * 3,
        out_shape=[out_shape] * 3,
        scratch_shapes=[pltpu.VMEM((nq, tm), F32)] + [pltpu.VMEM((PEER_NKEYS, tm), F32)] * 2
        + [pltpu.VMEM((k, tm), F32)] * 6,
        compiler_params=_cparams(("parallel", "arbitrary")),
        name="peer_select",
    )(x, g.reshape(1, d), shift, scale, wqt_hi, wqt_lo, sub_keys)


def _peer_dense_kernel(x_ref, g_ref, sh_ref, sc_ref, gt_ref, a_ref, b_ref, w_ref, u_ref, v_ref, out_ref,
                       f_ref, gs_ref, at_ref, bt_ref, wt_ref, p_ref, acc_ref):
    e = pl.program_id(1)
    tc = x_ref.shape[0]
    te = u_ref.shape[0]
    nk = PEER_NKEYS
    half = nk // 2
    npair = te // (2 * nk)
    hi_mask = jnp.int32(-65536)

    @pl.when(e == 0)
    def _():
        f_ref[...] = _norm_mod(x_ref[...], g_ref[...], sh_ref[...], sc_ref[...]).astype(BF16)
        acc_ref[...] = jnp.zeros(acc_ref.shape, F32)
        at_ref[...] = a_ref[...].T
        bt_ref[...] = b_ref[...].T
        wt_ref[...] = w_ref[...].T
        key = lax.broadcasted_iota(jnp.int32, (nk, nk), 0).astype(F32)

        def build(c, carry):
            first = jnp.where(key == at_ref[pl.ds(c, 1), :], 1.0, 0.0).astype(BF16)
            second = jnp.where(key == bt_ref[pl.ds(c, 1), :], wt_ref[pl.ds(c, 1), :], 0.0).astype(BF16)
            g = pltpu.bitcast(_dot_nt(first, second), jnp.int32) + 0x8000
            packed = (g[:half] & hi_mask) | lax.shift_right_logical(g[half:], jnp.full((half, nk), 16, jnp.int32))
            for k in range(half // 8):
                gs_ref[pl.ds(pl.multiple_of((k * tc + c) * 8, 8), 8), :] = packed[k * 8:(k + 1) * 8, :]
            return carry

        lax.fori_loop(0, tc, build, 0, unroll=32)

    for r in range(npair):
        p = e * npair + r
        packed = gs_ref[pl.ds((p // 8) * (tc * 8) + p % 8, tc, stride=8), :]
        g_a = pltpu.bitcast(packed & hi_mask, F32)
        g_b = pltpu.bitcast(packed << 16, F32)
        act = _dot_nt(f_ref[...], u_ref[r * 2 * nk:(r + 1) * 2 * nk, :])
        gelu = 0.5 * act * (1.0 + lax.erf(act * (2.0 ** -0.5)))
        p_ref[:, r * 2 * nk:r * 2 * nk + nk] = (g_a * gelu[:, :nk]).astype(BF16)
        p_ref[:, r * 2 * nk + nk:(r + 1) * 2 * nk] = (g_b * gelu[:, nk:]).astype(BF16)
    acc_ref[...] += _dot(p_ref[...], v_ref[...])

    @pl.when(e == pl.num_programs(1) - 1)
    def _():
        out_ref[...] = x_ref[...] + gt_ref[...] * acc_ref[...]


def _pair_permute(tab):
    half = PEER_NKEYS // 2
    t = tab.reshape(tab.shape[:-2] + (2, half, PEER_NKEYS, tab.shape[-1]))
    return jnp.swapaxes(t, -4, -3).reshape(tab.shape).astype(BF16)


def _peer_dense(x, g, shift, scale, gate, first, second, weight, u_tab, v_tab, tc, te, tiles_per_seg):
    n, d = x.shape
    n_exp = u_tab.shape[0]
    nhk = PEER_HEADS * PEER_TOPK
    mod_spec = lambda mod: pl.BlockSpec((None, mod.shape[1], d), lambda i, e: (i // tiles_per_seg, 0, 0))
    sel_spec = pl.BlockSpec((nhk, tc), lambda i, e: (0, i))
    return pl.pallas_call(
        _peer_dense_kernel,
        grid=(n // tc, n_exp // te),
        in_specs=[
            pl.BlockSpec((tc, d), lambda i, e: (i, 0)),
            pl.BlockSpec((1, d), lambda i, e: (0, 0)),
            mod_spec(shift), mod_spec(scale), mod_spec(gate),
            sel_spec, sel_spec, sel_spec,
            pl.BlockSpec((te, d), lambda i, e: (e, 0)),
            pl.BlockSpec((te, d), lambda i, e: (e, 0)),
        ],
        out_specs=pl.BlockSpec((tc, d), lambda i, e: (i, 0)),
        out_shape=jax.ShapeDtypeStruct((n, d), F32),
        scratch_shapes=[
            pltpu.VMEM((tc, d), BF16),
            pltpu.VMEM((tc * PEER_NKEYS // 2, PEER_NKEYS), jnp.int32),
            pltpu.VMEM((tc, nhk), F32),
            pltpu.VMEM((tc, nhk), F32),
            pltpu.VMEM((tc, nhk), F32),
            pltpu.VMEM((tc, te), BF16),
            pltpu.VMEM((tc, d), F32),
        ],
        compiler_params=_cparams(("parallel", "arbitrary")),
        name="peer_dense",
    )(x, g.reshape(1, d), shift, scale, gate, first, second, weight, u_tab, v_tab)


def _final_norm_kernel(x_ref, g_ref, o_ref):
    x = x_ref[...]
    ms = jnp.mean(x * x, axis=-1, keepdims=True)
    o_ref[...] = x * lax.rsqrt(ms + RMS_EPS) * g_ref[...]


def _final_norm(x, g, tm):
    n, d = x.shape
    return pl.pallas_call(
        _final_norm_kernel,
        grid=(n // tm,),
        in_specs=[pl.BlockSpec((tm, d), lambda i: (i, 0)), pl.BlockSpec((1, d), lambda i: (0, 0))],
        out_specs=pl.BlockSpec((tm, d), lambda i: (i, 0)),
        out_shape=jax.ShapeDtypeStruct((n, d), F32),
        compiler_params=_cparams(("parallel",)),
        name="final_norm",
    )(x, g.reshape(1, d))


TM = 256
PEER_TM = 512
PEER_TC = 512
PEER_TE = 2048


def _split_w(w):
    hi = w.astype(BF16)
    return hi, (w - hi.astype(F32)).astype(BF16)


def kernel(x_prompt, x_sample, cache_k_moba, cache_v_moba, state_kv_dil0, state_kv_dil1, state_kv_dil2, page_table,
           c_prompt, c_sample, w_ada, b_ada, g_norm, w_qkv, w_o, rel_bias, w_pq, peer_sub_keys, peer_u, peer_v,
           g_final):
    B, S, D = x_prompt.shape
    DB, T, _ = x_sample.shape
    n_a, n_pool = cache_k_moba.shape[:2]
    n_pages = page_table.shape[1]
    past_len = n_pages * PAGE_SIZE
    assert D == D_MODEL and S % MOBA_BLOCK == 0 and past_len % MOBA_BLOCK == 0 and S % TM == 0
    dil_states = (state_kv_dil0, state_kv_dil1, state_kv_dil2)
    for st, (win, dil) in zip(dil_states, DIL_GROUPS):
        assert st.shape[2] == win and win // dil == DIL_CHUNK and S % (dil * DIL_CHUNK) == 0

    np_, ns_ = B * S, DB * T
    tms = min(TM, ns_)
    hp = x_prompt.reshape(np_, D)
    hs = x_sample.reshape(ns_, D)

    mod_all = _ada(jnp.concatenate([c_prompt, c_sample], axis=0), w_ada, b_ada)
    wqkv_bf, wo_bf = w_qkv.astype(BF16), w_o.astype(BF16)
    wpqt_hi, wpqt_lo = _split_w(jnp.swapaxes(w_pq, 1, 2))
    u_bf, v_bf = _pair_permute(peer_u), _pair_permute(peer_v)
    pool_kt = cache_k_moba.transpose(0, 1, 3, 4, 2).reshape(n_a, n_pool, ATTN_DIM, PAGE_SIZE)
    pool_vt = cache_v_moba.transpose(0, 1, 3, 4, 2).reshape(n_a, n_pool, ATTN_DIM, PAGE_SIZE)

    bias_d = _bias_by_distance(rel_bias)
    moba_bias_p = _moba_prompt_bias(bias_d, S)
    moba_bias_s = _moba_sample_bias(bias_d, past_len, T)
    dil_bias_p = [_dil_prompt_bias(bias_d, g, w, d) for g, (w, d) in enumerate(DIL_GROUPS)]
    dil_bias_s = [_dil_sample_bias(bias_d, g, w, d, dil_states[g].shape[2], T)
                  for g, (w, d) in enumerate(DIL_GROUPS)]

    states_t = [st.transpose(0, 1, 3, 4, 5, 2).reshape(st.shape[0], DB, 2 * GROUP_DIM, st.shape[2])
                for st in dil_states]

    kp_rows, vp_rows, ks_rows, vs_rows = [], [], [], []
    dil_p = [[] for _ in DIL_GROUPS]
    dil_s = [None for _ in DIL_GROUPS]
    for i in range(DEPTH):
        j = i // 2
        mod = mod_all[i].reshape(B + DB, 6, D)
        mp = [mod[:B, c].reshape(B, 1, D) for c in range(6)]
        ms = [jnp.repeat(mod[B:, c], T, axis=0).reshape(ns_ // tms, tms, D) for c in range(6)]
        seg_p, seg_s = S // TM, 1

        qkv_p, kt_p, vt_p = _nmm(hp, g_norm[i, 0], mp[0], mp[1], wqkv_bf[i], TM, seg_p, kv_transposed=True)
        qkv_s = _nmm(hs, g_norm[i, 0], ms[0], ms[1], wqkv_bf[i], tms, seg_s)
        qkv_p3 = qkv_p.reshape(B, S, QKV_DIM)
        qkv_s3 = qkv_s.reshape(DB, T, QKV_DIM)
        qkv_s5 = qkv_s.reshape(DB, T, 3, N_HEADS, HEAD_DIM)
        if i % 2 == 0:
            attn_p = _moba_prompt(qkv_p3, moba_bias_p).reshape(np_, ATTN_DIM)
            attn_s = _moba_sample(qkv_s3, pool_kt, pool_vt, page_table, moba_bias_s, j, n_pages).reshape(ns_, ATTN_DIM)
            rows = lambda t: t.reshape(B, N_HEADS, HEAD_DIM, S).transpose(0, 3, 1, 2)
            kp_rows.append(rows(kt_p))
            vp_rows.append(rows(vt_p))
            ks_rows.append(qkv_s5[:, :, 1])
            vs_rows.append(qkv_s5[:, :, 2])
        else:
            op, lp, os_, ls = [], [], [], []
            for g, (win, dil) in enumerate(DIL_GROUPS):
                o, lse = _dil_prompt(qkv_p3, dil_bias_p[g], g, dil)
                op += [a.reshape(np_, a.shape[-1]) for a in o]
                lp += [a.reshape(np_, a.shape[-1]) for a in lse]
                o, lse, dil_s[g] = _dil_sample(qkv_s3, states_t[g], dil_s[g], *dil_bias_s[g], j, g)
                os_.append(o.reshape(ns_, GROUP_DIM))
                ls.append(lse.reshape(ns_, GROUP_DIM))
                keep = min(win, S)
                kv_p = jnp.stack([t[:, g * GROUP_DIM:(g + 1) * GROUP_DIM, S - keep:] for t in (kt_p, vt_p)], axis=1)
                dil_p[g].append(kv_p.reshape(B, 2, GROUP_HEADS, HEAD_DIM, keep).transpose(0, 4, 1, 2, 3))
            attn_p, attn_s = (op, lp), (os_, ls)
        hp = _oproj(hp, attn_p, mp[2], wo_bf[i], TM, seg_p)
        hs = _oproj(hs, attn_s, ms[2], wo_bf[i], tms, seg_s)

        tsel = min(PEER_TM, ns_)
        sel_p = _peer_select(hp, g_norm[i, 1], mp[3], mp[4], wpqt_hi[i], wpqt_lo[i], peer_sub_keys[i], PEER_TM,
                             S // PEER_TM)
        sel_s = _peer_select(hs, g_norm[i, 1], *[m.reshape(ns_ // tsel, tsel, D) for m in ms[3:5]],
                             wpqt_hi[i], wpqt_lo[i], peer_sub_keys[i], tsel, 1)
        hp = _peer_dense(hp, g_norm[i, 1], mp[3], mp[4], mp[5], *sel_p, u_bf[i], v_bf[i], PEER_TC, PEER_TE,
                         S // PEER_TC)
        hs = _peer_dense(hs, g_norm[i, 1], ms[3], ms[4], ms[5], *sel_s, u_bf[i], v_bf[i], tms, PEER_TE, 1)

    y_prompt = _final_norm(hp, g_final, TM).reshape(B, S, D)
    y_sample = _final_norm(hs, g_final, tms).reshape(DB, T, D)
    return (y_prompt, y_sample,
            jnp.stack(kp_rows), jnp.stack(vp_rows), jnp.stack(ks_rows), jnp.stack(vs_rows),
            jnp.stack(dil_p[0]), jnp.stack(dil_p[1]), jnp.stack(dil_p[2]),
            *[st.reshape(st.shape[:2] + (2, GROUP_HEADS, HEAD_DIM, st.shape[3])).transpose(0, 1, 5, 2, 3, 4)
              for st in dil_s])
```

```python
import functools
import math

import numpy as np
import jax
import jax.numpy as jnp
from jax import lax
from jax.experimental import pallas as pl
from jax.experimental.pallas import tpu as pltpu

F32 = jnp.float32
BF16 = jnp.bfloat16
NEG = -1e30

D_MODEL = 1024
N_HEADS = 12
HEAD_DIM = 64
ATTN_DIM = N_HEADS * HEAD_DIM
QKV_DIM = 3 * ATTN_DIM
ATTN_SCALE = HEAD_DIM ** -0.5
DEPTH = 4
PAGE_SIZE = 128
MOBA_BLOCK = 256
MOBA_TOPK = 3
DIL_GROUPS = ((128, 1), (512, 4), (2048, 16))
GROUP_HEADS = 4
GROUP_DIM = GROUP_HEADS * HEAD_DIM
DIL_CHUNK = 128
REL_BUCKETS = 32
REL_MAX_DIST = 2048
PEER_HEADS = 8
PEER_NKEYS = 128
PEER_DK = 256
PEER_TOPK = 16
RMS_EPS = 1e-6

VMEM_LIMIT = 56 * 1024 * 1024


def _cparams(sem):
    return pltpu.CompilerParams(dimension_semantics=sem, vmem_limit_bytes=VMEM_LIMIT)


def _split(x):
    hi = x.astype(BF16)
    lo = (x - hi.astype(F32)).astype(BF16)
    return hi, lo


def _dot(a, b):
    return jnp.dot(a, b, preferred_element_type=F32)


def _dot_nt(a, b):
    return lax.dot_general(a, b, (((1,), (1,)), ((), ())), preferred_element_type=F32)


def _dot3(a, b_hi, b_lo):
    a_hi, a_lo = _split(a)
    return _dot(a_hi, b_hi) + _dot(a_lo, b_hi) + _dot(a_hi, b_lo)


def _dot3_nt(a, b):
    a_hi, a_lo = _split(a)
    b_hi, b_lo = _split(b)
    return _dot_nt(a_hi, b_hi) + _dot_nt(a_lo, b_hi) + _dot_nt(a_hi, b_lo)


def _norm_mod(x, g, shift, scale):
    ms = jnp.mean(x * x, axis=-1, keepdims=True)
    xn = x * lax.rsqrt(ms + RMS_EPS) * g
    return xn * (1.0 + scale) + shift


def _bucket_np(dist):
    n = np.maximum(dist, 0)
    max_exact = REL_BUCKETS // 2
    nf = np.maximum(n, 1).astype(np.float64)
    large = max_exact + (np.log(nf / max_exact) / math.log(REL_MAX_DIST / max_exact)
                         * (REL_BUCKETS - max_exact)).astype(np.int64)
    large = np.minimum(large, REL_BUCKETS - 1)
    return np.where(n < max_exact, n, large).astype(np.int32)


BIAS_PAD = 256
BIAS_LEN = 4352


def _bias_by_distance(rel_bias):
    onehot = np.zeros((BIAS_LEN, REL_BUCKETS), np.float32)
    onehot[np.arange(BIAS_LEN), _bucket_np(np.arange(BIAS_LEN))] = 1.0
    by_dist = jnp.dot(jnp.asarray(onehot), rel_bias, precision=lax.Precision.HIGHEST)
    return jnp.concatenate([jnp.full((N_HEADS, BIAS_PAD), NEG, F32), by_dist.T], axis=1)


def _toeplitz(wp, rows, cols):
    n2 = wp.shape[-1]
    x = jnp.tile(wp, (1,) * (wp.ndim - 1) + (rows,))[..., :rows * (n2 - 1)]
    return x.reshape(wp.shape[:-1] + (rows, n2 - 1))[..., :cols]


def _rank_select(s, lane_n, n_valid, topk):
    nb = s.shape[1]
    rank = jnp.zeros(s.shape, F32)
    for m in range(nb):
        sm = s[:, m:m + 1]
        beats = (sm > s) | ((sm == s) & (m < lane_n))
        beats = beats & (m < n_valid)
        rank = rank + jnp.where(beats, 1.0, 0.0)
    sel = (lane_n < n_valid) & (rank < float(topk))
    return jnp.where(sel, 1.0, 0.0)


def _rank_select_rows(s, row_n, n_valid, topk):
    nb = s.shape[0]
    rank = jnp.zeros(s.shape, F32)
    for m in range(nb):
        sm = s[m:m + 1, :]
        beats = (sm > s) | ((sm == s) & (m < row_n))
        beats = beats & (m < n_valid)
        rank = rank + jnp.where(beats, 1.0, 0.0)
    sel = (row_n < n_valid) & (rank < float(topk))
    return jnp.where(sel, 1.0, 0.0)


def _ada_kernel(c_ref, w_ref, b_ref, o_ref):
    c = c_ref[...]
    s = c * (1.0 / (1.0 + jnp.exp(-c)))
    w_hi, w_lo = _split(w_ref[...])
    o_ref[...] = _dot3(s, w_hi, w_lo) + b_ref[...]


def _ada(c_all, w_ada, b_ada):
    n = c_all.shape[0]
    depth, d, n6 = w_ada.shape
    tn = 1536
    return pl.pallas_call(
        _ada_kernel,
        grid=(depth, n6 // tn),
        in_specs=[
            pl.BlockSpec((n, d), lambda i, j: (0, 0)),
            pl.BlockSpec((None, d, tn), lambda i, j: (i, 0, j)),
            pl.BlockSpec((None, 1, tn), lambda i, j: (i, 0, j)),
        ],
        out_specs=pl.BlockSpec((None, n, tn), lambda i, j: (i, 0, j)),
        out_shape=jax.ShapeDtypeStruct((depth, n, n6), F32),
        compiler_params=_cparams(("parallel", "parallel")),
        name="ada",
    )(c_all, w_ada, b_ada.reshape(depth, 1, n6))


def _nmm_kernel(x_ref, g_ref, sh_ref, sc_ref, w_ref, o_ref, *kvt_refs):
    f = _norm_mod(x_ref[...], g_ref[...], sh_ref[...], sc_ref[...])
    out = _dot(f.astype(BF16), w_ref[...])
    o_ref[...] = out
    for part, t_ref in enumerate(kvt_refs, start=1):
        t_ref[...] = out[:, part * ATTN_DIM:(part + 1) * ATTN_DIM].T


def _mod_spec(mod, tm, tiles_per_seg):
    r = mod.shape[1]
    return pl.BlockSpec((None, r, D_MODEL), lambda i, *_: (i // tiles_per_seg, 0, 0))


def _nmm(x, g, shift, scale, w, tm, tiles_per_seg, kv_transposed=False):
    n, d = x.shape
    nout = w.shape[1]
    out_specs = [pl.BlockSpec((tm, nout), lambda i: (i, 0))]
    out_shape = [jax.ShapeDtypeStruct((n, nout), F32)]
    if kv_transposed:
        n_seq = n // (tm * tiles_per_seg)
        t_spec = pl.BlockSpec((None, ATTN_DIM, tm), lambda i: (i // tiles_per_seg, 0, i % tiles_per_seg))
        out_specs += [t_spec, t_spec]
        out_shape += [jax.ShapeDtypeStruct((n_seq, ATTN_DIM, tm * tiles_per_seg), F32)] * 2
    res = pl.pallas_call(
        _nmm_kernel,
        grid=(n // tm,),
        in_specs=[
            pl.BlockSpec((tm, d), lambda i: (i, 0)),
            pl.BlockSpec((1, d), lambda i: (0, 0)),
            _mod_spec(shift, tm, tiles_per_seg),
            _mod_spec(scale, tm, tiles_per_seg),
            pl.BlockSpec((d, nout), lambda i: (0, 0)),
        ],
        out_specs=out_specs,
        out_shape=out_shape,
        compiler_params=_cparams(("parallel",)),
        name="nmm",
    )(x, g.reshape(1, d), shift, scale, w)
    return res if kv_transposed else res[0]


def _moba_prompt_kernel(q_ref, k_ref, v_ref, bias_ref, o_ref, kmean_ref, mask_ref):
    i = pl.program_id(2)
    blk = MOBA_BLOCK
    nb = k_ref.shape[0] // blk

    @pl.when(i == 0)
    def _():
        for n in range(nb):
            kmean_ref[n:n + 1, :] = jnp.mean(k_ref[n * blk:(n + 1) * blk, :], axis=0, keepdims=True)

    lane_head = lax.broadcasted_iota(jnp.int32, (blk, 128), 1) // HEAD_DIM
    row_n = lax.broadcasted_iota(jnp.int32, (nb, blk), 0)
    km_head = lax.broadcasted_iota(jnp.int32, (nb, 128), 1) // HEAD_DIM
    q2 = q_ref[...] * ATTN_SCALE
    qb = []
    for hh in range(2):
        q = jnp.where(lane_head == hh, q2, 0.0)
        scores = _dot3_nt(jnp.where(km_head == hh, kmean_ref[...], 0.0), q)
        sel = _rank_select_rows(scores, row_n, i, MOBA_TOPK)
        sel = jnp.where(row_n == i, 1.0, sel)
        sel = jnp.concatenate([sel, jnp.zeros((128 - nb, blk), F32)], axis=0).T
        for n in range(nb):
            mask_ref[hh, n] = jnp.broadcast_to(sel[:, n:n + 1], (blk, 128))
        mask_ref[hh, nb] = jnp.zeros((blk, 128), F32)
        qb.append(q.astype(BF16))

    def tile(j, hh):
        jc = jnp.maximum(j, 0)
        r0 = pl.multiple_of(jc * blk, blk)
        logits = _dot_nt(qb[hh], k_ref[pl.ds(r0, blk), :].astype(BF16)) + bias_ref[i - jc, hh]
        keep = mask_ref[hh, jnp.where(j < 0, nb, j)] > 0.5
        logits = jnp.where(jnp.concatenate([keep] * (blk // 128), axis=1), logits, NEG)
        return logits, v_ref[pl.ds(r0, blk), :].astype(BF16)

    state = [jnp.full((blk, 1), NEG, F32), jnp.zeros((blk, 1), F32), jnp.zeros((blk, 128), F32)] * 2

    def body(u, carry):
        out = []
        for hh in range(2):
            m, l, acc = carry[3 * hh:3 * hh + 3]
            lg0, v0 = tile(i - 2 * u, hh)
            lg1, v1 = tile(i - 2 * u - 1, hh)
            m_new = jnp.maximum(m, jnp.maximum(jnp.max(lg0, axis=1, keepdims=True), jnp.max(lg1, axis=1, keepdims=True)))
            alpha = jnp.exp(m - m_new)
            p0 = jnp.exp(lg0 - m_new)
            p1 = jnp.exp(lg1 - m_new)
            l = alpha * l + jnp.sum(p0, axis=1, keepdims=True) + jnp.sum(p1, axis=1, keepdims=True)
            acc = alpha * acc + _dot(p0.astype(BF16), v0) + _dot(p1.astype(BF16), v1)
            out += [m_new, l, acc]
        return tuple(out)

    _, l0, acc0, _, l1, acc1 = lax.fori_loop(0, i // 2 + 1, body, tuple(state))
    o_ref[...] = jnp.where(lane_head == 0, acc0 / l0, acc1 / l1)


def _moba_prompt(qkv, bias):
    b, s, _ = qkv.shape
    blk = MOBA_BLOCK
    nb = s // blk
    npair = N_HEADS // 2
    return pl.pallas_call(
        _moba_prompt_kernel,
        grid=(npair, b, nb),
        in_specs=[
            pl.BlockSpec((None, blk, 128), lambda hp, bb, i: (bb, i, hp)),
            pl.BlockSpec((None, s, 128), lambda hp, bb, i: (bb, 0, npair + hp)),
            pl.BlockSpec((None, s, 128), lambda hp, bb, i: (bb, 0, 2 * npair + hp)),
            pl.BlockSpec((nb, 2, blk, blk), lambda hp, bb, i: (0, hp, 0, 0)),
        ],
        out_specs=pl.BlockSpec((None, blk, 128), lambda hp, bb, i: (bb, i, hp)),
        out_shape=jax.ShapeDtypeStruct((b, s, ATTN_DIM), F32),
        scratch_shapes=[pltpu.VMEM((nb, 128), F32), pltpu.VMEM((2, nb + 1, blk, 128), F32)],
        compiler_params=_cparams(("parallel", "parallel", "arbitrary")),
        name="moba_prompt",
    )(qkv, qkv, qkv, bias)


def _moba_prompt_bias(bias_d, s):
    blk = MOBA_BLOCK
    nb = s // blk
    assert blk <= BIAS_PAD
    wps = []
    for delta in range(nb):
        z = BIAS_PAD + delta * blk
        lo = bias_d[:, z - blk:z + 1][:, ::-1]
        hi = bias_d[:, z + 1:z + blk][:, ::-1]
        wps.append(jnp.concatenate([lo, hi], axis=1))
    return _toeplitz(jnp.stack(wps), blk, blk)


def _moba_sample_kernel(pt_ref, new_ref, *refs, n_pages, t_new):
    del pt_ref
    kt_refs, vt_refs = refs[:n_pages], refs[n_pages:2 * n_pages]
    sbias_ref, o_ref, lg_ref, p_ref = refs[2 * n_pages:]
    rows = N_HEADS * t_new
    pages_per_block = MOBA_BLOCK // PAGE_SIZE
    n_blocks = n_pages // pages_per_block
    r_head = lax.broadcasted_iota(jnp.int32, (rows, ATTN_DIM), 0) // t_new
    c_head = lax.broadcasted_iota(jnp.int32, (rows, ATTN_DIM), 1) // HEAD_DIM
    pad = jnp.zeros((PAGE_SIZE - t_new, ATTN_DIM), F32)

    q = new_ref[:, 0:ATTN_DIM] * ATTN_SCALE
    qhi, qlo = _split(jnp.where(r_head == c_head, jnp.concatenate([q] * N_HEADS, axis=0), 0.0))

    scores = []
    for n in range(n_blocks):
        ksum = None
        for pg in range(n * pages_per_block, (n + 1) * pages_per_block):
            kt = kt_refs[pg][...]
            lg_ref[pg] = _dot(qhi, kt.astype(BF16))
            ksum = kt if ksum is None else ksum + kt
        k_hi, k_lo = _split(ksum)
        y = _dot(qhi, k_hi) + _dot(qlo, k_hi) + _dot(qhi, k_lo)
        scores.append(jnp.sum(y, axis=1, keepdims=True))
    s = jnp.concatenate(scores, axis=1)
    lane_n = lax.broadcasted_iota(jnp.int32, (rows, n_blocks), 1)
    sel = _rank_select(s, lane_n, n_blocks, MOBA_TOPK)
    knew = jnp.concatenate([new_ref[:, ATTN_DIM:2 * ATTN_DIM], pad], axis=0)
    lg_ref[n_pages] = _dot_nt(qhi, knew.astype(BF16))

    mrun = jnp.full((rows, PAGE_SIZE), NEG, F32)
    for pg in range(n_pages + 1):
        lgt = lg_ref[pg] + sbias_ref[pg]
        if pg < n_pages:
            n = pg // pages_per_block
            lgt = jnp.where(sel[:, n:n + 1] > 0.5, lgt, NEG)
        lg_ref[pg] = lgt
        mrun = jnp.maximum(mrun, lgt)
    m = jnp.max(mrun, axis=1, keepdims=True)
    srun = jnp.zeros((rows, PAGE_SIZE), F32)
    for pg in range(n_pages + 1):
        e = jnp.exp(lg_ref[pg] - m)
        srun = srun + e
        p_ref[pg] = e.astype(BF16)
    linv = 1.0 / jnp.sum(srun, axis=1, keepdims=True)

    vnew = jnp.concatenate([new_ref[:, 2 * ATTN_DIM:3 * ATTN_DIM], pad], axis=0)
    acc = _dot(p_ref[n_pages], vnew.astype(BF16))
    for pg in range(n_pages):
        acc = acc + _dot_nt(p_ref[pg], vt_refs[pg][...].astype(BF16))
    acc = jnp.where(r_head == c_head, acc * linv, 0.0)
    out = acc[0:t_new, :]
    for h in range(1, N_HEADS):
        out = out + acc[h * t_new:(h + 1) * t_new, :]
    o_ref[...] = out


def _moba_sample(qkv, pool_kt, pool_vt, page_table, sbias, layer, n_pages):
    db, t, _ = qkv.shape
    rows = N_HEADS * t
    page_spec = lambda pg: pl.BlockSpec((None, None, ATTN_DIM, PAGE_SIZE), lambda b, pt: (layer, pt[b, pg], 0, 0))
    grid_spec = pltpu.PrefetchScalarGridSpec(
        num_scalar_prefetch=1,
        grid=(db,),
        in_specs=[pl.BlockSpec((None, t, QKV_DIM), lambda b, pt: (b, 0, 0))]
        + [page_spec(pg) for pg in range(n_pages)] * 2
        + [pl.BlockSpec((n_pages + 1, rows, PAGE_SIZE), lambda b, pt: (0, 0, 0))],
        out_specs=pl.BlockSpec((None, t, ATTN_DIM), lambda b, pt: (b, 0, 0)),
        scratch_shapes=[
            pltpu.VMEM((n_pages + 1, rows, PAGE_SIZE), F32),
            pltpu.VMEM((n_pages + 1, rows, PAGE_SIZE), BF16),
        ],
    )
    return pl.pallas_call(
        functools.partial(_moba_sample_kernel, n_pages=n_pages, t_new=t),
        grid_spec=grid_spec,
        out_shape=jax.ShapeDtypeStruct((db, t, ATTN_DIM), F32),
        compiler_params=_cparams(("parallel",)),
        name="moba_sample",
    )(page_table, qkv, *([pool_kt] * n_pages), *([pool_vt] * n_pages), sbias)


def _moba_sample_bias(bias_d, past_len, t):
    n_pages = past_len // PAGE_SIZE
    z = BIAS_PAD
    assert PAGE_SIZE <= BIAS_PAD and past_len + t <= BIAS_LEN
    past = jnp.stack([bias_d[:, z + tq + 1:z + tq + 1 + past_len][:, ::-1] for tq in range(t)], axis=1)
    past = past.reshape(N_HEADS * t, n_pages, PAGE_SIZE).transpose(1, 0, 2)
    new = jnp.stack([bias_d[:, z + tq - (PAGE_SIZE - 1):z + tq + 1][:, ::-1] for tq in range(t)], axis=1)
    return jnp.concatenate([past, new.reshape(1, N_HEADS * t, PAGE_SIZE)], axis=0)


def _dil_prompt_kernel(*refs, dil):
    q_refs, k_refs, v_refs = refs[0:2], refs[2:4], refs[4:6]
    bias_ref = refs[6]
    o_refs, lse_refs = refs[7:9], refs[9:11]
    r, c = pl.program_id(1), pl.program_id(2)
    ch = DIL_CHUNK
    rows = lambda chunk: pl.ds(r + dil * ch * chunk, ch, stride=dil) if dil > 1 else pl.ds(ch * chunk, ch)
    cur, prev = rows(c), rows(jnp.maximum(c - 1, 0))
    col = lax.broadcasted_iota(jnp.int32, (ch, 2 * ch), 1)
    keep = (col >= ch) | (c > 0)
    lane_head = lax.broadcasted_iota(jnp.int32, (ch, 2 * HEAD_DIM), 1) // HEAD_DIM
    for pc in range(2):
        q = q_refs[pc][cur, :] * ATTN_SCALE
        k = jnp.concatenate([k_refs[pc][prev, :], k_refs[pc][cur, :]], axis=0).astype(BF16)
        v = jnp.concatenate([v_refs[pc][prev, :], v_refs[pc][cur, :]], axis=0).astype(BF16)
        o_h, lse_h = [], []
        for hh in range(2):
            qh = jnp.where(lane_head == hh, q, 0.0).astype(BF16)
            s = jnp.where(keep, _dot_nt(qh, k) + bias_ref[2 * pc + hh], NEG)
            m = jnp.max(s, axis=1, keepdims=True)
            p = jnp.exp(s - m)
            l = jnp.sum(p, axis=1, keepdims=True)
            o_h.append(_dot(p.astype(BF16), v) / l)
            lse_h.append(m + jnp.log(l))
        o_refs[pc][cur, :] = jnp.where(lane_head == 0, o_h[0], o_h[1])
        lse_refs[pc][cur, :] = jnp.where(lane_head == 0, lse_h[0], lse_h[1])


def _dil_prompt(qkv, bias, g, dil):
    b, s, _ = qkv.shape
    nch = s // dil // DIL_CHUNK
    pieces = ATTN_DIM // 128
    col = lambda part, pc: pl.BlockSpec((None, s, 128), lambda bb, r, c: (bb, 0, part * pieces + 2 * g + pc))
    out_spec = pl.BlockSpec((None, s, 128), lambda bb, r, c: (bb, 0, 0))
    res = pl.pallas_call(
        functools.partial(_dil_prompt_kernel, dil=dil),
        grid=(b, dil, nch),
        in_specs=[col(part, pc) for part in range(3) for pc in range(2)]
        + [pl.BlockSpec((GROUP_HEADS, DIL_CHUNK, 2 * DIL_CHUNK), lambda bb, r, c: (0, 0, 0))],
        out_specs=[out_spec] * 4,
        out_shape=[jax.ShapeDtypeStruct((b, s, 128), F32)] * 4,
        compiler_params=_cparams(("parallel", "arbitrary", "arbitrary")),
        name=f"dil_prompt{g}",
    )(*([qkv] * 6), bias)
    return res[0:2], res[2:4]


def _dil_prompt_bias(bias_d, g, win, dil):
    ch = DIL_CHUNK
    assert win // dil == ch
    z = BIAS_PAD
    heads = bias_d[g * GROUP_HEADS:(g + 1) * GROUP_HEADS]
    steps = heads[:, z:z + win + 1:dil][:, ::-1]
    wp = jnp.concatenate([steps, jnp.full((GROUP_HEADS, 4 * ch - (ch + 1)), NEG, F32)], axis=1)
    return _toeplitz(wp, ch, 2 * ch)


def _dil_sample_kernel(new_ref, buf_ref, bb_ref, bn_ref, *rest, g, t_new):
    o_ref, lse_ref, st_ref = rest[-3:]
    lb = buf_ref.shape[1]
    rows = GROUP_HEADS * t_new
    ch = DIL_CHUNK
    r_head = lax.broadcasted_iota(jnp.int32, (rows, GROUP_DIM), 0) // t_new
    c_head = lax.broadcasted_iota(jnp.int32, (rows, GROUP_DIM), 1) // HEAD_DIM
    diag = r_head == c_head
    pad = jnp.zeros((ch - t_new, GROUP_DIM), F32)
    qcol = g * GROUP_DIM
    kcol = ATTN_DIM + g * GROUP_DIM
    vcol = 2 * ATTN_DIM + g * GROUP_DIM
    knew, vnew = new_ref[:, kcol:kcol + GROUP_DIM], new_ref[:, vcol:vcol + GROUP_DIM]
    q = new_ref[:, qcol:qcol + GROUP_DIM] * ATTN_SCALE
    qbd = jnp.where(diag, jnp.concatenate([q] * GROUP_HEADS, axis=0), 0.0).astype(BF16)
    buf = buf_ref[...]
    kt, vt = buf[:GROUP_DIM].astype(BF16), buf[GROUP_DIM:].astype(BF16)
    lgb = _dot(qbd, kt) + bb_ref[...]
    lgn = _dot_nt(qbd, jnp.concatenate([knew, pad], axis=0).astype(BF16)) + bn_ref[...]
    m = jnp.maximum(jnp.max(lgb, axis=1, keepdims=True), jnp.max(lgn, axis=1, keepdims=True))
    pb = jnp.exp(lgb - m)
    pn = jnp.exp(lgn - m)
    l = jnp.sum(pb, axis=1, keepdims=True) + jnp.sum(pn, axis=1, keepdims=True)
    o = _dot_nt(pb.astype(BF16), vt) + _dot(pn.astype(BF16), jnp.concatenate([vnew, pad], axis=0).astype(BF16))
    o = jnp.where(diag, o / l, 0.0)
    lse = jnp.where(diag, m + jnp.log(l), 0.0)
    o_ref[...] = sum(o[h * t_new:(h + 1) * t_new] for h in range(GROUP_HEADS))
    lse_ref[...] = sum(lse[h * t_new:(h + 1) * t_new] for h in range(GROUP_HEADS))

    tail = jnp.concatenate([jnp.zeros((ch - t_new, 2 * GROUP_DIM), F32),
                            jnp.concatenate([knew, vnew], axis=1)], axis=0).T
    shifted = pltpu.roll(buf, lb - t_new, axis=1)
    lane = lax.broadcasted_iota(jnp.int32, (2 * GROUP_DIM, ch), 1)
    if lb > ch:
        st_ref[:, 0:lb - ch] = shifted[:, 0:lb - ch]
    st_ref[:, lb - ch:lb] = jnp.where(lane >= ch - t_new, tail, shifted[:, lb - ch:lb])


def _dil_sample(qkv, state_t, prev_out, bb, bn, layer, g):
    db, t, _ = qkv.shape
    nb, _, kv_dim, lb = state_t.shape
    rows = GROUP_HEADS * t
    assert t <= DIL_CHUNK <= lb
    out_spec = pl.BlockSpec((None, t, GROUP_DIM), lambda b: (b, 0, 0))
    st_spec = pl.BlockSpec((None, None, kv_dim, lb), lambda b: (layer, b, 0, 0))
    args = [qkv, state_t, bb, bn]
    in_specs = [
        pl.BlockSpec((None, t, QKV_DIM), lambda b: (b, 0, 0)),
        st_spec,
        pl.BlockSpec((rows, lb), lambda b: (0, 0)),
        pl.BlockSpec((rows, DIL_CHUNK), lambda b: (0, 0)),
    ]
    aliases = {}
    if prev_out is not None:
        args.append(prev_out)
        in_specs.append(pl.BlockSpec(memory_space=pl.ANY))
        aliases = {4: 2}
    return pl.pallas_call(
        functools.partial(_dil_sample_kernel, g=g, t_new=t),
        grid=(db,),
        in_specs=in_specs,
        out_specs=[out_spec, out_spec, st_spec],
        out_shape=[jax.ShapeDtypeStruct((db, t, GROUP_DIM), F32)] * 2 + [jax.ShapeDtypeStruct(state_t.shape, F32)],
        input_output_aliases=aliases,
        compiler_params=_cparams(("parallel",)),
        name=f"dil_sample{g}",
    )(*args)


def _dil_sample_bias(bias_d, g, win, dil, lb, t_new):
    z = BIAS_PAD
    assert DIL_CHUNK <= BIAS_PAD and lb + t_new <= BIAS_LEN
    heads = bias_d[g * GROUP_HEADS:(g + 1) * GROUP_HEADS]
    bb, bn = [], []
    pos = np.arange(lb)
    tk = np.arange(DIL_CHUNK)
    for t in range(t_new):
        dist = lb + t - pos
        vals = heads[:, z + t + 1:z + t + 1 + lb][:, ::-1]
        bb.append(jnp.where(jnp.asarray((dist % dil == 0) & (dist <= win)), vals, NEG))
        vals = heads[:, z + t - (DIL_CHUNK - 1):z + t + 1][:, ::-1]
        ok = (tk < t_new) & (t - tk >= 0) & ((t - tk) % dil == 0) & (t - tk <= win)
        bn.append(jnp.where(jnp.asarray(ok), vals, NEG))
    as_rows = lambda tabs: jnp.stack(tabs, axis=1).reshape(GROUP_HEADS * t_new, -1)
    return as_rows(bb), as_rows(bn)


def _oproj_kernel(h_ref, o_ref, gate_ref, w_ref, out_ref):
    out_ref[...] = h_ref[...] + gate_ref[...] * _dot(o_ref[...].astype(BF16), w_ref[...])


def _oproj_dil_kernel(h_ref, *refs):
    n_groups = len(DIL_GROUPS)
    gate_ref, w_ref, out_ref = refs[-3:]
    npc = (len(refs) - 3) // (2 * n_groups)
    o_refs, l_refs = refs[:n_groups * npc], refs[n_groups * npc:2 * n_groups * npc]
    scaled = [None] * (n_groups * npc)
    for pc in range(npc):
        lses = [l_refs[g * npc + pc][...] for g in range(n_groups)]
        mx = functools.reduce(jnp.maximum, lses)
        es = [jnp.exp(l - mx) for l in lses]
        inv = 1.0 / sum(es[1:], es[0])
        for g in range(n_groups):
            scaled[g * npc + pc] = o_refs[g * npc + pc][...] * (es[g] * inv)
    o = jnp.concatenate(scaled, axis=1)
    out_ref[...] = h_ref[...] + gate_ref[...] * _dot(o.astype(BF16), w_ref[...])


def _oproj(h, attn, gate, w, tm, tiles_per_seg):
    n, d = h.shape
    row = lambda w: pl.BlockSpec((tm, w), lambda i: (i, 0))
    if isinstance(attn, tuple):
        kern, attn_args = _oproj_dil_kernel, list(attn[0]) + list(attn[1])
        attn_specs = [row(a.shape[1]) for a in attn_args]
    else:
        kern, attn_args, attn_specs = _oproj_kernel, [attn], [row(ATTN_DIM)]
    return pl.pallas_call(
        kern,
        grid=(n // tm,),
        in_specs=[row(d)] + attn_specs + [
            _mod_spec(gate, tm, tiles_per_seg),
            pl.BlockSpec((ATTN_DIM, d), lambda i: (0, 0)),
        ],
        out_specs=row(d),
        out_shape=jax.ShapeDtypeStruct((n, d), F32),
        compiler_params=_cparams(("parallel",)),
        name="oproj",
    )(h, *attn_args, gate, w)


def _topk_rows(jobs, ridx, k):
    nrows = ridx.shape[0]

    def body(it, carry):
        for src_ref, val_ref, idx_ref in jobs:
            a = src_ref[0:nrows, :]
            mx = jnp.max(a, axis=0, keepdims=True)
            ix = jnp.min(jnp.where(a == mx, ridx, 1e9), axis=0, keepdims=True)
            src_ref[0:nrows, :] = jnp.where(ridx == ix, -jnp.inf, a)
            val_ref[pl.ds(it, 1), :] = mx
            idx_ref[pl.ds(it, 1), :] = ix
        return carry

    lax.fori_loop(0, k, body, 0)


def _cand_rows(k):
    return [(a, k // (a + 1)) for a in range(k)]


def _peer_select_kernel(x_ref, g_ref, sh_ref, sc_ref, whi_ref, wlo_ref, sk_ref, a_ref, b_ref, gate_ref,
                        qt_ref, s_ref, s2_ref, t1_ref, i1_ref, t2_ref, i2_ref, tc_ref, ic_ref):
    h = pl.program_id(1)
    k = PEER_TOPK
    half = PEER_DK // 2
    tm = x_ref.shape[0]

    @pl.when(h == 0)
    def _():
        f_hi, f_lo = _split(_norm_mod(x_ref[...], g_ref[...], sh_ref[...], sc_ref[...]))
        whi = whi_ref[...]
        qt_ref[...] = _dot_nt(whi, f_hi) + _dot_nt(wlo_ref[...], f_hi) + _dot_nt(whi, f_lo)

    key_rows = lax.broadcasted_iota(jnp.int32, (PEER_NKEYS, tm), 0).astype(F32)
    row0 = pl.multiple_of(h * PEER_DK, PEER_DK)
    for part, sub_ref in enumerate((s_ref, s2_ref)):
        q_hi, q_lo = _split(qt_ref[pl.ds(row0 + part * half, half), :])
        sub_ref[...] = _dot3(sk_ref[part], q_hi, q_lo)
    _topk_rows([(s_ref, t1_ref, i1_ref), (s2_ref, t2_ref, i2_ref)], key_rows, k)
    off, flat = 0, []
    for a, nb in _cand_rows(k):
        s_ref[off:off + nb, :] = t1_ref[a:a + 1, :] + t2_ref[0:nb, :]
        flat.append(lax.broadcasted_iota(jnp.int32, (nb, tm), 0).astype(F32) + float(a * k))
        off += nb
    n_cand = -(-off // 8) * 8
    s_ref[off:n_cand, :] = jnp.full((n_cand - off, tm), -jnp.inf, F32)
    flat.append(jnp.full((n_cand - off, tm), float(k * k), F32))
    _topk_rows([(s_ref, tc_ref, ic_ref)], jnp.concatenate(flat, axis=0), k)
    jc = ic_ref[...]
    ja = jnp.floor(jc * (1.0 / k))
    jb = jc - ja * k
    e1 = jnp.zeros(jc.shape, F32)
    e2 = jnp.zeros(jc.shape, F32)
    for a in range(k):
        e1 = e1 + jnp.where(ja == float(a), i1_ref[a:a + 1, :], 0.0)
        e2 = e2 + jnp.where(jb == float(a), i2_ref[a:a + 1, :], 0.0)
    top = tc_ref[...]
    e = jnp.exp(top - top[0:1, :])
    a_ref[...] = e1
    b_ref[...] = e2
    gate_ref[...] = e / jnp.sum(e, axis=0, keepdims=True)


def _peer_select(x, g, shift, scale, wqt_hi, wqt_lo, sub_keys, tm, tiles_per_seg):
    n, d = x.shape
    k = PEER_TOPK
    nq = PEER_HEADS * PEER_DK
    out_spec = pl.BlockSpec((k, tm), lambda i, h: (h, i))
    out_shape = jax.ShapeDtypeStruct((PEER_HEADS * k, n), F32)
    mod_spec = lambda mod: pl.BlockSpec((None, mod.shape[1], d), lambda i, h: (i // tiles_per_seg, 0, 0))
    return pl.pallas_call(
        _peer_select_kernel,
        grid=(n // tm, PEER_HEADS),
        in_specs=[
            pl.BlockSpec((tm, d), lambda i, h: (i, 0)),
            pl.BlockSpec((1, d), lambda i, h: (0, 0)),
            mod_spec(shift),
            mod_spec(scale),
            pl.BlockSpec((nq, d), lambda i, h: (0, 0)),
            pl.BlockSpec((nq, d), lambda i, h: (0, 0)),
            pl.BlockSpec((2, PEER_NKEYS, PEER_DK // 2), lambda i, h: (0, 0, 0)),
        ],
        out_specs=[out_spec] * 3,
        out_shape=[out_shape] * 3,
        scratch_shapes=[pltpu.VMEM((nq, tm), F32)] + [pltpu.VMEM((PEER_NKEYS, tm), F32)] * 2
        + [pltpu.VMEM((k, tm), F32)] * 6,
        compiler_params=_cparams(("parallel", "arbitrary")),
        name="peer_select",
    )(x, g.reshape(1, d), shift, scale, wqt_hi, wqt_lo, sub_keys)


def _peer_dense_kernel(x_ref, g_ref, sh_ref, sc_ref, gt_ref, a_ref, b_ref, w_ref, u_ref, v_ref, out_ref,
                       f_ref, gs_ref, at_ref, bt_ref, wt_ref, p_ref, acc_ref):
    e = pl.program_id(1)
    tc = x_ref.shape[0]
    te = u_ref.shape[0]
    nk = PEER_NKEYS
    half = nk // 2
    npair = te // (2 * nk)
    hi_mask = jnp.int32(-65536)

    @pl.when(e == 0)
    def _():
        f_ref[...] = _norm_mod(x_ref[...], g_ref[...], sh_ref[...], sc_ref[...]).astype(BF16)
        acc_ref[...] = jnp.zeros(acc_ref.shape, F32)
        at_ref[...] = a_ref[...].T
        bt_ref[...] = b_ref[...].T
        wt_ref[...] = w_ref[...].T
        key = lax.broadcasted_iota(jnp.int32, (nk, nk), 0).astype(F32)

        def build(c, carry):
            first = jnp.where(key == at_ref[pl.ds(c, 1), :], 1.0, 0.0).astype(BF16)
            second = jnp.where(key == bt_ref[pl.ds(c, 1), :], wt_ref[pl.ds(c, 1), :], 0.0).astype(BF16)
            g = pltpu.bitcast(_dot_nt(first, second), jnp.int32) + 0x8000
            packed = (g[:half] & hi_mask) | lax.shift_right_logical(g[half:], jnp.full((half, nk), 16, jnp.int32))
            for k in range(half // 8):
                gs_ref[pl.ds(pl.multiple_of((k * tc + c) * 8, 8), 8), :] = packed[k * 8:(k + 1) * 8, :]
            return carry

        lax.fori_loop(0, tc, build, 0, unroll=64)

    for r in range(npair):
        p = e * npair + r
        packed = gs_ref[pl.ds((p // 8) * (tc * 8) + p % 8, tc, stride=8), :]
        g_a = pltpu.bitcast(packed & hi_mask, F32)
        g_b = pltpu.bitcast(packed << 16, F32)
        act = _dot_nt(f_ref[...], u_ref[r * 2 * nk:(r + 1) * 2 * nk, :])
        gelu = 0.5 * act * (1.0 + lax.erf(act * (2.0 ** -0.5)))
        p_ref[:, r * 2 * nk:r * 2 * nk + nk] = (g_a * gelu[:, :nk]).astype(BF16)
        p_ref[:, r * 2 * nk + nk:(r + 1) * 2 * nk] = (g_b * gelu[:, nk:]).astype(BF16)
    acc_ref[...] += _dot(p_ref[...], v_ref[...])

    @pl.when(e == pl.num_programs(1) - 1)
    def _():
        out_ref[...] = x_ref[...] + gt_ref[...] * acc_ref[...]


def _pair_permute(tab):
    half = PEER_NKEYS // 2
    t = tab.reshape(tab.shape[:-2] + (2, half, PEER_NKEYS, tab.shape[-1]))
    return jnp.swapaxes(t, -4, -3).reshape(tab.shape).astype(BF16)


def _peer_dense(x, g, shift, scale, gate, first, second, weight, u_tab, v_tab, tc, te, tiles_per_seg):
    n, d = x.shape
    n_exp = u_tab.shape[0]
    nhk = PEER_HEADS * PEER_TOPK
    mod_spec = lambda mod: pl.BlockSpec((None, mod.shape[1], d), lambda i, e: (i // tiles_per_seg, 0, 0))
    sel_spec = pl.BlockSpec((nhk, tc), lambda i, e: (0, i))
    return pl.pallas_call(
        _peer_dense_kernel,
        grid=(n // tc, n_exp // te),
        in_specs=[
            pl.BlockSpec((tc, d), lambda i, e: (i, 0)),
            pl.BlockSpec((1, d), lambda i, e: (0, 0)),
            mod_spec(shift), mod_spec(scale), mod_spec(gate),
            sel_spec, sel_spec, sel_spec,
            pl.BlockSpec((te, d), lambda i, e: (e, 0)),
            pl.BlockSpec((te, d), lambda i, e: (e, 0)),
        ],
        out_specs=pl.BlockSpec((tc, d), lambda i, e: (i, 0)),
        out_shape=jax.ShapeDtypeStruct((n, d), F32),
        scratch_shapes=[
            pltpu.VMEM((tc, d), BF16),
            pltpu.VMEM((tc * PEER_NKEYS // 2, PEER_NKEYS), jnp.int32),
            pltpu.VMEM((tc, nhk), F32),
            pltpu.VMEM((tc, nhk), F32),
            pltpu.VMEM((tc, nhk), F32),
            pltpu.VMEM((tc, te), BF16),
            pltpu.VMEM((tc, d), F32),
        ],
        compiler_params=_cparams(("parallel", "arbitrary")),
        name="peer_dense",
    )(x, g.reshape(1, d), shift, scale, gate, first, second, weight, u_tab, v_tab)


def _final_norm_kernel(x_ref, g_ref, o_ref):
    x = x_ref[...]
    ms = jnp.mean(x * x, axis=-1, keepdims=True)
    o_ref[...] = x * lax.rsqrt(ms + RMS_EPS) * g_ref[...]


def _final_norm(x, g, tm):
    n, d = x.shape
    return pl.pallas_call(
        _final_norm_kernel,
        grid=(n // tm,),
        in_specs=[pl.BlockSpec((tm, d), lambda i: (i, 0)), pl.BlockSpec((1, d), lambda i: (0, 0))],
        out_specs=pl.BlockSpec((tm, d), lambda i: (i, 0)),
        out_shape=jax.ShapeDtypeStruct((n, d), F32),
        compiler_params=_cparams(("parallel",)),
        name="final_norm",
    )(x, g.reshape(1, d))


TM = 256
PEER_TM = 512
PEER_TC = 512
PEER_TE = 2048


def _split_w(w):
    hi = w.astype(BF16)
    return hi, (w - hi.astype(F32)).astype(BF16)


def kernel(x_prompt, x_sample, cache_k_moba, cache_v_moba, state_kv_dil0, state_kv_dil1, state_kv_dil2, page_table,
           c_prompt, c_sample, w_ada, b_ada, g_norm, w_qkv, w_o, rel_bias, w_pq, peer_sub_keys, peer_u, peer_v,
           g_final):
    B, S, D = x_prompt.shape
    DB, T, _ = x_sample.shape
    n_a, n_pool = cache_k_moba.shape[:2]
    n_pages = page_table.shape[1]
    past_len = n_pages * PAGE_SIZE
    assert D == D_MODEL and S % MOBA_BLOCK == 0 and past_len % MOBA_BLOCK == 0 and S % TM == 0
    dil_states = (state_kv_dil0, state_kv_dil1, state_kv_dil2)
    for st, (win, dil) in zip(dil_states, DIL_GROUPS):
        assert st.shape[2] == win and win // dil == DIL_CHUNK and S % (dil * DIL_CHUNK) == 0

    np_, ns_ = B * S, DB * T
    tms = min(TM, ns_)
    hp = x_prompt.reshape(np_, D)
    hs = x_sample.reshape(ns_, D)

    mod_all = _ada(jnp.concatenate([c_prompt, c_sample], axis=0), w_ada, b_ada)
    wqkv_bf, wo_bf = w_qkv.astype(BF16), w_o.astype(BF16)
    wpqt_hi, wpqt_lo = _split_w(jnp.swapaxes(w_pq, 1, 2))
    u_bf, v_bf = _pair_permute(peer_u), _pair_permute(peer_v)
    pool_kt = cache_k_moba.transpose(0, 1, 3, 4, 2).reshape(n_a, n_pool, ATTN_DIM, PAGE_SIZE)
    pool_vt = cache_v_moba.transpose(0, 1, 3, 4, 2).reshape(n_a, n_pool, ATTN_DIM, PAGE_SIZE)

    bias_d = _bias_by_distance(rel_bias)
    moba_bias_p = _moba_prompt_bias(bias_d, S)
    moba_bias_s = _moba_sample_bias(bias_d, past_len, T)
    dil_bias_p = [_dil_prompt_bias(bias_d, g, w, d) for g, (w, d) in enumerate(DIL_GROUPS)]
    dil_bias_s = [_dil_sample_bias(bias_d, g, w, d, dil_states[g].shape[2], T)
                  for g, (w, d) in enumerate(DIL_GROUPS)]

    states_t = [st.transpose(0, 1, 3, 4, 5, 2).reshape(st.shape[0], DB, 2 * GROUP_DIM, st.shape[2])
                for st in dil_states]

    kp_rows, vp_rows, ks_rows, vs_rows = [], [], [], []
    dil_p = [[] for _ in DIL_GROUPS]
    dil_s = [None for _ in DIL_GROUPS]
    for i in range(DEPTH):
        j = i // 2
        mod = mod_all[i].reshape(B + DB, 6, D)
        mp = [mod[:B, c].reshape(B, 1, D) for c in range(6)]
        ms = [jnp.repeat(mod[B:, c], T, axis=0).reshape(ns_ // tms, tms, D) for c in range(6)]
        seg_p, seg_s = S // TM, 1

        qkv_p, kt_p, vt_p = _nmm(hp, g_norm[i, 0], mp[0], mp[1], wqkv_bf[i], TM, seg_p, kv_transposed=True)
        qkv_s = _nmm(hs, g_norm[i, 0], ms[0], ms[1], wqkv_bf[i], tms, seg_s)
        qkv_p3 = qkv_p.reshape(B, S, QKV_DIM)
        qkv_s3 = qkv_s.reshape(DB, T, QKV_DIM)
        qkv_s5 = qkv_s.reshape(DB, T, 3, N_HEADS, HEAD_DIM)
        if i % 2 == 0:
            attn_p = _moba_prompt(qkv_p3, moba_bias_p).reshape(np_, ATTN_DIM)
            attn_s = _moba_sample(qkv_s3, pool_kt, pool_vt, page_table, moba_bias_s, j, n_pages).reshape(ns_, ATTN_DIM)
            rows = lambda t: t.reshape(B, N_HEADS, HEAD_DIM, S).transpose(0, 3, 1, 2)
            kp_rows.append(rows(kt_p))
            vp_rows.append(rows(vt_p))
            ks_rows.append(qkv_s5[:, :, 1])
            vs_rows.append(qkv_s5[:, :, 2])
        else:
            op, lp, os_, ls = [], [], [], []
            for g, (win, dil) in enumerate(DIL_GROUPS):
                o, lse = _dil_prompt(qkv_p3, dil_bias_p[g], g, dil)
                op += [a.reshape(np_, a.shape[-1]) for a in o]
                lp += [a.reshape(np_, a.shape[-1]) for a in lse]
                o, lse, dil_s[g] = _dil_sample(qkv_s3, states_t[g], dil_s[g], *dil_bias_s[g], j, g)
                os_.append(o.reshape(ns_, GROUP_DIM))
                ls.append(lse.reshape(ns_, GROUP_DIM))
                keep = min(win, S)
                kv_p = jnp.stack([t[:, g * GROUP_DIM:(g + 1) * GROUP_DIM, S - keep:] for t in (kt_p, vt_p)], axis=1)
                dil_p[g].append(kv_p.reshape(B, 2, GROUP_HEADS, HEAD_DIM, keep).transpose(0, 4, 1, 2, 3))
            attn_p, attn_s = (op, lp), (os_, ls)
        hp = _oproj(hp, attn_p, mp[2], wo_bf[i], TM, seg_p)
        hs = _oproj(hs, attn_s, ms[2], wo_bf[i], tms, seg_s)

        tsel = min(PEER_TM, ns_)
        sel_p = _peer_select(hp, g_norm[i, 1], mp[3], mp[4], wpqt_hi[i], wpqt_lo[i], peer_sub_keys[i], PEER_TM,
                             S // PEER_TM)
        sel_s = _peer_select(hs, g_norm[i, 1], *[m.reshape(ns_ // tsel, tsel, D) for m in ms[3:5]],
                             wpqt_hi[i], wpqt_lo[i], peer_sub_keys[i], tsel, 1)
        hp = _peer_dense(hp, g_norm[i, 1], mp[3], mp[4], mp[5], *sel_p, u_bf[i], v_bf[i], PEER_TC, PEER_TE,
                         S // PEER_TC)
        hs = _peer_dense(hs, g_norm[i, 1], ms[3], ms[4], ms[5], *sel_s, u_bf[i], v_bf[i], tms, PEER_TE, 1)

    y_prompt = _final_norm(hp, g_final, TM).reshape(B, S, D)
    y_sample = _final_norm(hs, g_final, tms).reshape(DB, T, D)
    return (y_prompt, y_sample,
            jnp.stack(kp_rows), jnp.stack(vp_rows), jnp.stack(ks_rows), jnp.stack(vs_rows),
            jnp.stack(dil_p[0]), jnp.stack(dil_p[1]), jnp.stack(dil_p[2]),
            *[st.reshape(st.shape[:2] + (2, GROUP_HEADS, HEAD_DIM, st.shape[3])).transpose(0, 1, 5, 2, 3, 4)
              for st in dil_s])
```
